```python
import math
import jax, jax.numpy as jnp
from jax import lax
import numpy as np

D_MODEL = 1024
BATCH = 1
SEQ = 16384
DEPTH = 2
DEC_BATCH = 128
DEC_SEQ = 1
PAST_LEN = 16384
PAGE_SIZE = 128

GLA_H = 4
GLA_DK = 32
GLA_DV = 64
GLA_GATE_RANK = 16
GLA_GATE_TAU = 16.0
GLA_CHUNK = 64
RET_H = 4
RET_DK = 32
RET_DV = 64
RET_CHUNK = 64
MLA_H = 4
MLA_DQ = 256
MLA_DC = 128
MLA_DR = 32
MLA_NOPE = 64
MLA_DV = 64
MLA_SCALE = (MLA_NOPE + MLA_DR) ** -0.5
SB_H = 4
SB_DH = 64
N_BRANCH = 4
BRANCH_W = 256
D_FF = 2816
CONV_W = 3
Q_BLOCK = 128
ROPE_BASE = 10000.0
EPS = 1e-6

IN_SIZES = (GLA_H * GLA_DK, GLA_H * GLA_DK, GLA_H * GLA_DV, GLA_GATE_RANK, GLA_H * GLA_DV,
            RET_H * RET_DK, RET_H * RET_DK, RET_H * RET_DV, RET_H * RET_DV,
            MLA_DQ, MLA_DC, MLA_DR,
            SB_H * SB_DH, SB_H * SB_DH, SB_H * SB_DH,
            N_BRANCH * D_MODEL)
D_IN = sum(IN_SIZES)

kernel_name = 'hybrid_gla_ret_mla_sb_step'


def split_points():
    pts, acc = [], 0
    for s in IN_SIZES[:-1]:
        acc += s
        pts.append(acc)
    return pts


def rms_norm(x, g):
    xf = x.astype(jnp.float32)
    y = xf * lax.rsqrt(jnp.mean(xf * xf, axis=-1, keepdims=True) + EPS) * g.astype(jnp.float32)
    return y.astype(x.dtype)


def head_rms(o, g):
    of = o.astype(jnp.float32)
    return of * lax.rsqrt(jnp.mean(of * of, axis=-1, keepdims=True) + EPS) * g.astype(jnp.float32)


def head_groupnorm(o, g):
    of = o.astype(jnp.float32)
    mu = jnp.mean(of, axis=-1, keepdims=True)
    c = of - mu
    return c * lax.rsqrt(jnp.mean(c * c, axis=-1, keepdims=True) + EPS) * g.astype(jnp.float32)


def rope(x, pos):
    half = x.shape[-1] // 2
    inv_freq = jnp.exp(-math.log(ROPE_BASE) * jnp.arange(half, dtype=jnp.float32) / half)
    ang = pos.astype(jnp.float32)[:, None] * inv_freq[None, :]
    shape = (1, pos.shape[0]) + (1,) * (x.ndim - 3) + (half,)
    cos = jnp.cos(ang).reshape(shape)
    sin = jnp.sin(ang).reshape(shape)
    xf = x.astype(jnp.float32)
    x1, x2 = xf[..., :half], xf[..., half:]
    return jnp.concatenate([x1 * cos - x2 * sin, x2 * cos + x1 * sin], axis=-1).astype(x.dtype)


def map_query_blocks(fn, q_args, q_pos):
    L = q_pos.shape[0]
    if L <= Q_BLOCK:
        return fn(q_args, q_pos)
    qb = math.gcd(L, Q_BLOCK)
    nb = L // qb

    def to_blocks(a):
        return jnp.moveaxis(a.reshape((a.shape[0], nb, qb) + a.shape[2:]), 1, 0)

    out = lax.map(lambda xs: fn(xs[0], xs[1]), (tuple(to_blocks(a) for a in q_args), q_pos.reshape(nb, qb)))
    out = jnp.moveaxis(out, 0, 1)
    return out.reshape((out.shape[0], L) + out.shape[3:])


def gla_recurrence(q, k, v, log_a, s0):
    B, L, H, dk = q.shape
    dv = v.shape[-1]
    C = math.gcd(L, GLA_CHUNK)
    n = L // C
    f32 = jnp.float32

    def chunks(a):
        return jnp.moveaxis(a.astype(f32).reshape((B, n, C) + a.shape[2:]), 1, 0)

    causal = jnp.tril(jnp.ones((C, C), dtype=bool))[None, :, :, None, None]

    def step(S, inp):
        qc, kc, vc, ac = inp
        b = jnp.cumsum(ac, axis=1)
        o_inter = jnp.einsum('bihk,bhkv->bihv', qc * jnp.exp(b), S)
        rel = jnp.where(causal, b[:, :, None] - b[:, None, :], -jnp.inf)
        scores = jnp.einsum('bihk,bjhk,bijhk->bhij', qc, kc, jnp.exp(rel))
        o_intra = jnp.einsum('bhij,bjhv->bihv', scores, vc)
        b_end = b[:, -1]
        S_new = jnp.exp(b_end)[..., None] * S + jnp.einsum('bjhk,bjhv->bhkv', kc * jnp.exp(b_end[:, None] - b), vc)
        return S_new, o_inter + o_intra

    S, o = lax.scan(step, s0.astype(f32), (chunks(q), chunks(k), chunks(v), chunks(log_a)))
    return jnp.moveaxis(o, 0, 1).reshape(B, L, H, dv), S


def retention_recurrence(q, k, v, s0):
    B, L, H, dk = q.shape
    dv = v.shape[-1]
    C = math.gcd(L, RET_CHUNK)
    n = L // C
    f32 = jnp.float32
    log_g = jnp.log1p(-jnp.exp2(-5.0 - jnp.arange(H, dtype=f32)))
    idx = jnp.arange(C, dtype=f32)
    rel = idx[:, None] - idx[None, :]
    decay = jnp.exp(jnp.where(rel[None] >= 0, rel[None] * log_g[:, None, None], -jnp.inf))
    q_dec = jnp.exp((idx[:, None] + 1.0) * log_g[None, :])
    k_dec = jnp.exp((C - 1.0 - idx[:, None]) * log_g[None, :])
    chunk_dec = jnp.exp(C * log_g)

    def chunks(a):
        return jnp.moveaxis(a.astype(f32).reshape((B, n, C) + a.shape[2:]), 1, 0)

    def step(S, inp):
        qc, kc, vc = inp
        o_inter = jnp.einsum('bihk,bhkv->bihv', qc, S) * q_dec[None, :, :, None]
        scores = jnp.einsum('bihk,bjhk->bhij', qc, kc) * decay[None]
        o_intra = jnp.einsum('bhij,bjhv->bihv', scores, vc)
        S_new = chunk_dec[None, :, None, None] * S + jnp.einsum('bjhk,bjhv->bhkv', kc * k_dec[None, :, :, None], vc)
        return S_new, o_inter + o_intra

    S, o = lax.scan(step, s0.astype(f32), (chunks(q), chunks(k), chunks(v)))
    return jnp.moveaxis(o, 0, 1).reshape(B, L, H, dv), S


def mla_attend(q_lat, q_rope, c_kv, k_rope, q_pos, k_pos):
    s = (jnp.einsum('bqhc,bkc->bhqk', q_lat, c_kv) + jnp.einsum('bqhr,bkr->bhqk', q_rope, k_rope)).astype(jnp.float32) * MLA_SCALE
    mask = k_pos[None, :] <= q_pos[:, None]
    p = jax.nn.softmax(jnp.where(mask, s, -jnp.inf), axis=-1)
    return jnp.einsum('bhqk,bkc->bqhc', p.astype(c_kv.dtype), c_kv)


def stick_breaking(q, k, v, q_pos, k_pos):
    z = jnp.einsum('bqhd,bkhd->bhqk', q, k).astype(jnp.float32) * (SB_DH ** -0.5)
    mask = k_pos[None, :] < q_pos[:, None]
    log_1mb = jnp.where(mask, jax.nn.log_sigmoid(-z), 0.0)
    suffix = lax.cumsum(log_1mb, axis=3, reverse=True) - log_1mb
    a = jnp.exp(jnp.where(mask, jax.nn.log_sigmoid(z) + suffix, -jnp.inf))
    return jnp.einsum('bhqk,bkhd->bqhd', a.astype(v.dtype), v)


def token_mixers(h, lw, past_ckv, past_krope, past_k, past_v, s_gla, s_ret):
    B, L, _ = h.shape
    P = past_ckv.shape[1]
    f32 = jnp.float32
    q_pos = P + jnp.arange(L, dtype=jnp.int32)
    k_pos = jnp.arange(P + L, dtype=jnp.int32)
    (g_q, g_k, g_v, g_lr, g_r, r_q, r_k, r_v, r_g,
     m_cq, m_ckv, m_kr, s_q, s_k, s_v, gate_logits) = jnp.split(h @ lw['w_in'], split_points(), axis=-1)

    qa = g_q.reshape(B, L, GLA_H, GLA_DK) * (GLA_DK ** -0.5)
    ka = g_k.reshape(B, L, GLA_H, GLA_DK)
    va = g_v.reshape(B, L, GLA_H, GLA_DV)
    log_a = (jax.nn.log_sigmoid((g_lr @ lw['gla_w_gate2'] + lw['gla_b_gate']).astype(f32)) / GLA_GATE_TAU).reshape(B, L, GLA_H, GLA_DK)
    o_a, new_gla = gla_recurrence(qa, ka, va, log_a, s_gla)
    o_a = head_rms(o_a, lw['gla_norm']) * jax.nn.silu(g_r.astype(f32)).reshape(B, L, GLA_H, GLA_DV)
    o_a = o_a.reshape(B, L, BRANCH_W).astype(h.dtype)

    qr = rope(r_q.reshape(B, L, RET_H, RET_DK), q_pos)
    kr = rope(r_k.reshape(B, L, RET_H, RET_DK), q_pos) * (RET_DK ** -0.5)
    vr = r_v.reshape(B, L, RET_H, RET_DV)
    o_b, new_ret = retention_recurrence(qr, kr, vr, s_ret)
    o_b = head_groupnorm(o_b, lw['ret_norm']) * jax.nn.silu(r_g.astype(f32)).reshape(B, L, RET_H, RET_DV)
    o_b = o_b.reshape(B, L, BRANCH_W).astype(h.dtype)

    c_q = rms_norm(m_cq, lw['mla_norm_q'])
    q_c = (c_q @ lw['mla_w_uq']).reshape(B, L, MLA_H, MLA_NOPE + MLA_DR)
    q_nope = q_c[..., :MLA_NOPE]
    q_rope = rope(q_c[..., MLA_NOPE:], q_pos)
    c_kv = rms_norm(m_ckv, lw['mla_norm_kv'])
    k_rope = rope(m_kr, q_pos)
    q_lat = jnp.einsum('blhn,chn->blhc', q_nope, lw['mla_w_uk'])
    ckv_all = jnp.concatenate([past_ckv.astype(c_kv.dtype), c_kv], axis=1)
    kr_all = jnp.concatenate([past_krope.astype(k_rope.dtype), k_rope], axis=1)
    o_lat = map_query_blocks(lambda qs, qp: mla_attend(qs[0], qs[1], ckv_all, kr_all, qp, k_pos), (q_lat, q_rope), q_pos)
    o_c = jnp.einsum('blhc,chv->blhv', o_lat, lw['mla_w_uv']).reshape(B, L, BRANCH_W)

    sq = s_q.reshape(B, L, SB_H, SB_DH)
    sk = s_k.reshape(B, L, SB_H, SB_DH)
    sv = s_v.reshape(B, L, SB_H, SB_DH)
    k_all = jnp.concatenate([past_k.astype(sk.dtype), sk], axis=1)
    v_all = jnp.concatenate([past_v.astype(sv.dtype), sv], axis=1)
    o_d = map_query_blocks(lambda qs, qp: stick_breaking(qs[0], k_all, v_all, qp, k_pos), (sq,), q_pos)
    o_d = o_d.reshape(B, L, BRANCH_W)

    branches = jnp.stack([o_a, o_b, o_c, o_d], axis=2)
    proj = jnp.einsum('blnc,ncd->blnd', branches, lw['w_branch'])
    gates = jax.nn.sigmoid(gate_logits.astype(f32)).reshape(B, L, N_BRANCH, D_MODEL)
    merged = jnp.sum(gates * proj.astype(f32), axis=2).astype(h.dtype)
    return merged @ lw['w_out'], (c_kv, k_rope, sk, sv, new_gla, new_ret)


def conv_ffn(h, w_in, conv_w, conv_b, w_out, conv_state):
    L = h.shape[1]
    gate_pre, up = jnp.split(h @ w_in, 2, axis=-1)
    ext = jnp.concatenate([conv_state.astype(gate_pre.dtype), gate_pre], axis=1)
    conv = conv_b + conv_w[0] * ext[:, 0:L]
    for i in range(1, CONV_W):
        conv = conv + conv_w[i] * ext[:, i:i + L]
    y = (jax.nn.gelu(conv, approximate=False) * up) @ w_out
    return y, ext[:, L:]


def decoder_layer(x, lw, past_ckv, past_krope, past_k, past_v, s_gla, s_ret, s_conv):
    mix, rows = token_mixers(rms_norm(x, lw['norm_mix_pre']), lw, past_ckv, past_krope, past_k, past_v, s_gla, s_ret)
    x = x + rms_norm(mix, lw['norm_mix_post'])
    f, conv_new = conv_ffn(rms_norm(x, lw['norm_ffn_pre']), lw['w_ffn_in'], lw['ffn_conv_w'], lw['ffn_conv_b'], lw['w_ffn_out'], s_conv)
    x = x + rms_norm(f, lw['norm_ffn_post'])
    return x, rows + (conv_new,)


def gather_pages(pool, page_table):
    g = jnp.take(pool, page_table, axis=0)
    return g.reshape((g.shape[0], g.shape[1] * g.shape[2]) + g.shape[3:])


def setup_inputs(seed: int = 0) -> dict:
    key = jax.random.key(seed)
    kit = iter(jax.random.split(key, 40))
    f32 = jnp.float32
    n_pages = PAST_LEN // PAGE_SIZE
    n_pool = (DEC_BATCH * n_pages * 5) // 4

    def nrm(shape, scale):
        return jax.random.normal(next(kit), shape, f32) * scale

    def gain(shape):
        return 1.0 + nrm(shape, 0.02)

    page_table = jax.random.permutation(next(kit), n_pool)[:DEC_BATCH * n_pages].reshape(DEC_BATCH, n_pages).astype(jnp.int32)
    return {
        'x_prompt': nrm((BATCH, SEQ, D_MODEL), 1.0),
        'x_sample': nrm((DEC_BATCH, DEC_SEQ, D_MODEL), 1.0),
        'cache_mla_ckv': nrm((DEPTH, n_pool, PAGE_SIZE, MLA_DC), 1.0),
        'cache_mla_krope': nrm((DEPTH, n_pool, PAGE_SIZE, MLA_DR), 1.0),
        'cache_sb_k': nrm((DEPTH, n_pool, PAGE_SIZE, SB_H, SB_DH), 1.0),
        'cache_sb_v': nrm((DEPTH, n_pool, PAGE_SIZE, SB_H, SB_DH), 1.0),
        'state_gla': nrm((DEPTH, DEC_BATCH, GLA_H, GLA_DK, GLA_DV), 1.0),
        'state_ret': nrm((DEPTH, DEC_BATCH, RET_H, RET_DK, RET_DV), 1.0),
        'state_ffn_conv': nrm((DEPTH, DEC_BATCH, CONV_W - 1, D_FF), 1.0),
        'page_table': page_table,
        'w_in': nrm((DEPTH, D_MODEL, D_IN), D_MODEL ** -0.5),
        'gla_w_gate2': nrm((DEPTH, GLA_GATE_RANK, GLA_H * GLA_DK), GLA_GATE_RANK ** -0.5),
        'gla_b_gate': nrm((DEPTH, GLA_H * GLA_DK), 0.1),
        'gla_norm': gain((DEPTH, GLA_H, GLA_DV)),
        'ret_norm': gain((DEPTH, RET_H, RET_DV)),
        'mla_norm_q': gain((DEPTH, MLA_DQ)),
        'mla_norm_kv': gain((DEPTH, MLA_DC)),
        'mla_w_uq': nrm((DEPTH, MLA_DQ, MLA_H * (MLA_NOPE + MLA_DR)), MLA_DQ ** -0.5),
        'mla_w_uk': nrm((DEPTH, MLA_DC, MLA_H, MLA_NOPE), MLA_DC ** -0.5),
        'mla_w_uv': nrm((DEPTH, MLA_DC, MLA_H, MLA_DV), MLA_DC ** -0.5),
        'w_branch': nrm((DEPTH, N_BRANCH, BRANCH_W, D_MODEL), BRANCH_W ** -0.5),
        'w_out': nrm((DEPTH, D_MODEL, D_MODEL), D_MODEL ** -0.5),
        'w_ffn_in': nrm((DEPTH, D_MODEL, 2 * D_FF), D_MODEL ** -0.5),
        'ffn_conv_w': nrm((DEPTH, CONV_W, D_FF), CONV_W ** -0.5),
        'ffn_conv_b': nrm((DEPTH, D_FF), 0.01),
        'w_ffn_out': nrm((DEPTH, D_FF, D_MODEL), D_FF ** -0.5),
        'norm_mix_pre': gain((DEPTH, D_MODEL)),
        'norm_mix_post': gain((DEPTH, D_MODEL)),
        'norm_ffn_pre': gain((DEPTH, D_MODEL)),
        'norm_ffn_post': gain((DEPTH, D_MODEL)),
    }


def reference(x_prompt, x_sample, cache_mla_ckv, cache_mla_krope, cache_sb_k, cache_sb_v, state_gla, state_ret,
              state_ffn_conv, page_table, w_in, gla_w_gate2, gla_b_gate, gla_norm, ret_norm, mla_norm_q, mla_norm_kv,
              mla_w_uq, mla_w_uk, mla_w_uv, w_branch, w_out, w_ffn_in, ffn_conv_w, ffn_conv_b, w_ffn_out,
              norm_mix_pre, norm_mix_post, norm_ffn_pre, norm_ffn_post):
    dt = x_prompt.dtype
    f32 = jnp.float32
    xp, xs = x_prompt, x_sample
    Bp = xp.shape[0]
    prompt_new = [[] for _ in range(7)]
    sample_new = [[] for _ in range(7)]
    for l in range(DEPTH):
        lw = {'w_in': w_in[l], 'gla_w_gate2': gla_w_gate2[l], 'gla_b_gate': gla_b_gate[l], 'gla_norm': gla_norm[l],
              'ret_norm': ret_norm[l], 'mla_norm_q': mla_norm_q[l], 'mla_norm_kv': mla_norm_kv[l],
              'mla_w_uq': mla_w_uq[l], 'mla_w_uk': mla_w_uk[l], 'mla_w_uv': mla_w_uv[l], 'w_branch': w_branch[l],
              'w_out': w_out[l], 'w_ffn_in': w_ffn_in[l], 'ffn_conv_w': ffn_conv_w[l], 'ffn_conv_b': ffn_conv_b[l],
              'w_ffn_out': w_ffn_out[l], 'norm_mix_pre': norm_mix_pre[l], 'norm_mix_post': norm_mix_post[l],
              'norm_ffn_pre': norm_ffn_pre[l], 'norm_ffn_post': norm_ffn_post[l]}
        xp, rows_p = decoder_layer(
            xp, lw,
            jnp.zeros((Bp, 0, MLA_DC), dt), jnp.zeros((Bp, 0, MLA_DR), dt),
            jnp.zeros((Bp, 0, SB_H, SB_DH), dt), jnp.zeros((Bp, 0, SB_H, SB_DH), dt),
            jnp.zeros((Bp, GLA_H, GLA_DK, GLA_DV), f32), jnp.zeros((Bp, RET_H, RET_DK, RET_DV), f32),
            jnp.zeros((Bp, CONV_W - 1, D_FF), dt))
        xs, rows_s = decoder_layer(
            xs, lw,
            gather_pages(cache_mla_ckv[l], page_table), gather_pages(cache_mla_krope[l], page_table),
            gather_pages(cache_sb_k[l], page_table), gather_pages(cache_sb_v[l], page_table),
            state_gla[l], state_ret[l], state_ffn_conv[l])
        for i in range(7):
            prompt_new[i].append(rows_p[i])
            sample_new[i].append(rows_s[i])
    p_ckv, p_krope, p_sb_k, p_sb_v, p_gla, p_ret, p_conv = [jnp.stack(a, axis=0) for a in prompt_new]
    s_ckv, s_krope, s_sb_k, s_sb_v, s_gla, s_ret, s_conv = [jnp.stack(a, axis=0) for a in sample_new]
    return (xp, xs, p_ckv, p_krope, p_sb_k, p_sb_v, p_gla, p_ret, p_conv,
            s_ckv, s_krope, s_sb_k, s_sb_v, s_gla, s_ret, s_conv)
```

```python
import functools
import math

import jax
import jax.numpy as jnp
from jax import lax
from jax.experimental import pallas as pl
from jax.experimental.pallas import tpu as pltpu

F32 = jnp.float32
BF16 = jnp.bfloat16

N_HEAD = 4
REC_DK = 32
REC_DV = 64
GLA_RANK = 16
GLA_TAU = 16.0
MLA_DQ = 256
MLA_DC = 128
MLA_DR = 32
MLA_NOPE = 64
MLA_DV = 64
MLA_SCALE = (MLA_NOPE + MLA_DR) ** -0.5
SB_DH = 64
N_BRANCH = 4
BRANCH_W = 256
CONV_W = 3
ROPE_BASE = 10000.0
EPS = 1e-6
REC_CHUNK = 64
SUB = 16
SB_DEAD = -104.0

LANE = 128
VMEM_LIMIT = 52 * 1024 * 1024

C_GQK, C_GV, C_GR, C_RQK, C_RV, C_RG, C_MCQ, C_SQ, C_SK, C_SV = range(10)
C_MCKV = 20
C_TAIL = 21
D_MIX = 22 * LANE
TAIL_GLR = MLA_DR


def _cparams(sem):
    return pltpu.CompilerParams(dimension_semantics=sem, vmem_limit_bytes=VMEM_LIMIT)


def _dot(a, b):
    return jnp.dot(a, b, preferred_element_type=F32)


def _dot_nt(a, b):
    return lax.dot_general(a, b, (((1,), (1,)), ((), ())), preferred_element_type=F32)


def _dot_tn(a, b):
    return lax.dot_general(a, b, (((0,), (0,)), ((), ())), preferred_element_type=F32)


def _split3(x):
    hi = x.astype(BF16)
    r1 = x - hi.astype(F32)
    mid = r1.astype(BF16)
    lo = (r1 - mid.astype(F32)).astype(BF16)
    return hi, mid, lo


def _dot_exact_rhs(x, m):
    hi, mid, lo = _split3(x)
    return _dot(hi, m) + _dot(mid, m) + _dot(lo, m)


def _dot_exact_lhs(m, x):
    hi, mid, lo = _split3(x)
    return _dot(m, hi) + _dot(m, mid) + _dot(m, lo)


def _log_sigmoid_pair(z):
    t = jnp.log1p(jnp.exp(-jnp.abs(z)))
    return jnp.minimum(z, 0.0) - t, -jnp.maximum(z, 0.0) - t


def _rms(x, g):
    return x * lax.rsqrt(jnp.mean(x * x, axis=-1, keepdims=True) + EPS) * g


def _iota(shape, dim):
    return lax.broadcasted_iota(jnp.int32, shape, dim)


def _rope_lanes(x, cos, sin_signed):
    lane = _iota(x.shape, 1)
    swapped = jnp.where((lane % 32) < 16, pltpu.roll(x, 112, 1), pltpu.roll(x, 16, 1))
    return x * cos + swapped * sin_signed


def _head_mask(shape, lane_w, h):
    lane = _iota(shape, len(shape) - 1)
    return (lane >= h * lane_w) & (lane < (h + 1) * lane_w)


def _norm_matmul_kernel(x_ref, g_ref, w_ref, o_ref, h_ref):
    @pl.when(pl.program_id(1) == 0)
    def _():
        h_ref[...] = _rms(x_ref[...], g_ref[...]).astype(BF16)

    o_ref[...] = _dot(h_ref[...], w_ref[...])


def _norm_matmul(x, g, w):
    m, d = x.shape
    n = w.shape[1]
    tm = min(m, 1024)
    tn = n // 2
    return pl.pallas_call(
        _norm_matmul_kernel,
        grid=(m // tm, n // tn),
        in_specs=[pl.BlockSpec((tm, d), lambda i, j: (i, 0)),
                  pl.BlockSpec((1, d), lambda i, j: (0, 0)),
                  pl.BlockSpec((d, tn), lambda i, j: (0, j))],
        out_specs=pl.BlockSpec((tm, tn), lambda i, j: (i, j)),
        out_shape=jax.ShapeDtypeStruct((m, n), F32),
        scratch_shapes=[pltpu.VMEM((tm, d), BF16)],
        compiler_params=_cparams(("arbitrary", "arbitrary")),
        name="norm_matmul",
    )(x, g, w)


def _block_diag_mask():
    r = _iota((N_HEAD * REC_DK, N_HEAD * REC_DV), 0) // REC_DK
    c = _iota((N_HEAD * REC_DK, N_HEAD * REC_DV), 1) // REC_DV
    return r == c


def _head_avg_matrix():
    r = _iota((N_HEAD * REC_DV, N_HEAD * REC_DV), 0) // REC_DV
    c = _iota((N_HEAD * REC_DV, N_HEAD * REC_DV), 1) // REC_DV
    return jnp.where(r == c, 1.0 / REC_DV, 0.0).astype(BF16)


def _head_rms_lanes(o, g, avg):
    ms = _dot_exact_rhs(o * o, avg)
    return o * lax.rsqrt(ms + EPS) * g


def _head_groupnorm_lanes(o, g, avg):
    c = o - _dot_exact_rhs(o, avg)
    var = _dot_exact_rhs(c * c, avg)
    return c * lax.rsqrt(var + EPS) * g


def _silu(x):
    return x * jax.nn.sigmoid(x)


def _stack_heads(x, lane_w):
    return jnp.concatenate(
        [jnp.where(_head_mask(x.shape, lane_w, h), x, 0.0) for h in range(N_HEAD)], axis=0)


def _unstack_heads(o, rows, lane_w):
    acc = None
    for h in range(N_HEAD):
        blk = o[h * rows:(h + 1) * rows, :]
        term = jnp.where(_head_mask(blk.shape, lane_w, h), blk, 0.0)
        acc = term if acc is None else acc + term
    return acc


def _gla_chunk(q, k, v, log_a, s_gla, tri, expand, bmask):
    c = REC_CHUNK
    b = _dot_exact_lhs(tri, log_a)
    vb = v.astype(BF16)
    o = _dot((q * jnp.exp(b)).astype(BF16), s_gla.astype(BF16))
    row = _iota((SUB, N_HEAD * REC_DK), 0)
    parts = []
    for blk in range(c // SUB):
        lo = blk * SUB
        q_i, b_i = q[lo:lo + SUB], b[lo:lo + SUB]
        prods = []
        for j in range(SUB):
            rel = jnp.where(row >= j, b_i - b[lo + j:lo + j + 1], -jnp.inf)
            prods.append(q_i * k[lo + j:lo + j + 1] * jnp.exp(rel))
        p_all = jnp.concatenate(prods, axis=0).astype(BF16)
        r_all = _dot(p_all, expand)
        o_i = None
        for j in range(SUB):
            term = r_all[j * SUB:(j + 1) * SUB] * v[lo + j:lo + j + 1]
            o_i = term if o_i is None else o_i + term
        if blk > 0:
            ref = b[lo - 1:lo]
            q_t = q_i * jnp.exp(b_i - ref)
            k_rows = _iota((c, N_HEAD * REC_DK), 0)
            k_t = jnp.where(k_rows < lo, k * jnp.exp(jnp.minimum(ref - b, 0.0)), 0.0)
            s = _dot_nt(_stack_heads(q_t, REC_DK).astype(BF16), k_t.astype(BF16))
            o_i = o_i + _unstack_heads(_dot(s.astype(BF16), vb), SUB, REC_DV)
        parts.append(o_i)
    o = o + jnp.concatenate(parts, axis=0)
    b_end = b[c - 1:c]
    decay_rows = jnp.broadcast_to(jnp.exp(b_end), (LANE, LANE)).T
    decay_rows = jnp.concatenate([decay_rows, decay_rows], axis=1)
    upd = _dot_tn((k * jnp.exp(b_end - b)).astype(BF16), vb)
    s_new = decay_rows * s_gla + jnp.where(bmask, upd, 0.0)
    return o, s_new


def _ret_chunk(q, k, v, s_ret, dmat, qdec, kdec, cdec, bmask):
    c = REC_CHUNK
    vb = v.astype(BF16)
    qb = q.astype(BF16)
    o_inter = _dot(qb, s_ret.astype(BF16)) * qdec
    s = _dot_nt(_stack_heads(q, REC_DK).astype(BF16), k.astype(BF16)) * dmat
    o_intra = _unstack_heads(_dot(s.astype(BF16), vb), c, REC_DV)
    upd = _dot_tn((k * kdec).astype(BF16), vb)
    s_new = cdec * s_ret + jnp.where(bmask, upd, 0.0)
    return o_inter + o_intra, s_new


def _rec_prompt_kernel(gqk_ref, gv_ref, gr_ref, rqk_ref, rv_ref, rg_ref, tail_ref, cos_ref, sin_ref,
                       w2_ref, bg_ref, gn_ref, rn_ref, dmat_ref, qdec_ref, kdec_ref, cdec_ref,
                       o_ref, sg_out_ref, sr_out_ref, sg_ref, sr_ref, *, tc):
    step = pl.program_id(0)

    @pl.when(step == 0)
    def _():
        sg_ref[...] = jnp.zeros_like(sg_ref)
        sr_ref[...] = jnp.zeros_like(sr_ref)

    c = REC_CHUNK
    tri = (_iota((c, c), 0) >= _iota((c, c), 1)).astype(BF16)
    bmask = _block_diag_mask()
    expand = bmask.astype(BF16)
    avg = _head_avg_matrix()
    dk = N_HEAD * REC_DK

    def chunk(ci, carry):
        rows = pl.ds(pl.multiple_of(ci * c, c), c)
        gqk = gqk_ref[rows, :]
        x = _dot(tail_ref[rows, :].astype(BF16), w2_ref[...]) + bg_ref[...]
        log_a = _log_sigmoid_pair(x)[0] * (1.0 / GLA_TAU)
        o_g, sg_new = _gla_chunk(gqk[:, :dk] * (REC_DK ** -0.5), gqk[:, dk:], gv_ref[rows, :], log_a,
                                 sg_ref[...], tri, expand, bmask)
        sg_ref[...] = sg_new
        o_a = _head_rms_lanes(o_g, gn_ref[...], avg) * _silu(gr_ref[rows, :])

        rqk = rqk_ref[rows, :]
        cos, sin = cos_ref[rows, :], sin_ref[rows, :]
        rq = _rope_lanes(rqk[:, :dk], cos, sin)
        rk = _rope_lanes(rqk[:, dk:], cos, sin) * (REC_DK ** -0.5)
        o_r, sr_new = _ret_chunk(rq, rk, rv_ref[rows, :], sr_ref[...], dmat_ref[...], qdec_ref[...],
                                 kdec_ref[...], cdec_ref[...], bmask)
        sr_ref[...] = sr_new
        o_b = _head_groupnorm_lanes(o_r, rn_ref[...], avg) * _silu(rg_ref[rows, :])
        o_ref[rows, :] = jnp.concatenate([o_a, o_b], axis=1).astype(BF16)
        return carry

    lax.fori_loop(0, tc // c, chunk, 0)

    @pl.when(step == pl.num_programs(0) - 1)
    def _():
        for h in range(N_HEAD):
            sg_out_ref[h] = sg_ref[h * REC_DK:(h + 1) * REC_DK, h * REC_DV:(h + 1) * REC_DV]
            sr_out_ref[h] = sr_ref[h * REC_DK:(h + 1) * REC_DK, h * REC_DV:(h + 1) * REC_DV]


def _ret_tables():
    c = REC_CHUNK
    log_g = jnp.log1p(-jnp.exp2(-5.0 - jnp.arange(N_HEAD, dtype=F32)))
    idx = jnp.arange(c, dtype=F32)
    rel = idx[:, None] - idx[None, :]
    decay = jnp.exp(jnp.where(rel[None] >= 0, rel[None] * log_g[:, None, None], -jnp.inf))
    q_dec = jnp.exp((idx[:, None] + 1.0) * log_g[None, :])
    k_dec = jnp.exp((c - 1.0 - idx[:, None]) * log_g[None, :])
    chunk_dec = jnp.exp(c * log_g)
    dmat = decay.reshape(N_HEAD * c, c)
    qdec = jnp.repeat(q_dec, REC_DV, axis=1)
    kdec = jnp.repeat(k_dec, REC_DK, axis=1)
    cdec = jnp.broadcast_to(jnp.repeat(chunk_dec, REC_DK)[:, None], (N_HEAD * REC_DK, N_HEAD * REC_DV))
    return dmat, qdec, kdec, cdec, jnp.exp(log_g)


def _rec_prompt(proj, cos, sin, lw):
    l = proj.shape[0]
    tc = min(l, 256)
    dmat, qdec, kdec, cdec, _ = _ret_tables()
    blk256 = lambda j: pl.BlockSpec((tc, 256), lambda i, j=j: (i, j))
    blk128 = lambda j: pl.BlockSpec((tc, LANE), lambda i, j=j: (i, j))
    row128 = pl.BlockSpec((tc, LANE), lambda i: (i, 0))
    full = lambda a: pl.BlockSpec(a.shape, lambda i: (0,) * a.ndim)
    consts = (lw['w2'], lw['gla_b'], lw['gla_norm'], lw['ret_norm'], dmat, qdec, kdec, cdec)
    state = jax.ShapeDtypeStruct((N_HEAD, REC_DK, REC_DV), F32)
    return pl.pallas_call(
        functools.partial(_rec_prompt_kernel, tc=tc),
        grid=(l // tc,),
        in_specs=[blk256(C_GQK), blk256(C_GV), blk256(C_GR), blk256(C_RQK), blk256(C_RV), blk256(C_RG),
                  blk128(C_TAIL), row128, row128] + [full(a) for a in consts],
        out_specs=[pl.BlockSpec((tc, 2 * BRANCH_W), lambda i: (i, 0)),
                   pl.BlockSpec(state.shape, lambda i: (0, 0, 0)),
                   pl.BlockSpec(state.shape, lambda i: (0, 0, 0))],
        out_shape=[jax.ShapeDtypeStruct((l, 2 * BRANCH_W), BF16), state, state],
        scratch_shapes=[pltpu.VMEM((N_HEAD * REC_DK, N_HEAD * REC_DV), F32),
                        pltpu.VMEM((N_HEAD * REC_DK, N_HEAD * REC_DV), F32)],
        compiler_params=_cparams(("arbitrary",)),
        name="rec_prompt",
    )(proj, proj, proj, proj, proj, proj, proj, cos, sin, *consts)


def _rec_sample_kernel(proj_ref, cos_ref, sin_ref, w2_ref, bg_ref, gn_ref, rn_ref, rdec_ref,
                       sg_in_ref, sr_in_ref, o_ref, sg_out_ref, sr_out_ref,
                       a_ref, k_ref, q_ref, v_ref, og_ref, or_ref):
    dk = N_HEAD * REC_DK
    dv = N_HEAD * REC_DV
    col = lambda j, w=256: proj_ref[:, j * w:(j + 1) * w]
    gqk, rqk = col(C_GQK), col(C_RQK)
    x = _dot(col(C_TAIL, LANE).astype(BF16), w2_ref[...]) + bg_ref[...]
    a_gla = jnp.exp(_log_sigmoid_pair(x)[0] * (1.0 / GLA_TAU))
    cos, sin = cos_ref[...], sin_ref[...]
    rq = _rope_lanes(rqk[:, :dk], cos, sin)
    rk = _rope_lanes(rqk[:, dk:], cos, sin) * (REC_DK ** -0.5)

    def run(idx, a_t, k_t, q_t, v_t, s_in_ref, s_out_ref, oacc_ref):
        a_ref[idx], k_ref[idx], q_ref[idx], v_ref[idx] = a_t, k_t, q_t, v_t
        oacc_ref[...] = jnp.zeros_like(oacc_ref)

        def body(hk, carry):
            h = hk // REC_DK
            srow = pl.ds(pl.multiple_of(hk * REC_DV, REC_DV), REC_DV)
            vrow = pl.ds(pl.multiple_of(h * REC_DV, REC_DV), REC_DV)
            s_new = (a_ref[idx, pl.ds(hk, 1), :] * s_in_ref[srow, :]
                     + k_ref[idx, pl.ds(hk, 1), :] * v_ref[idx, vrow, :])
            s_out_ref[srow, :] = s_new
            oacc_ref[vrow, :] += q_ref[idx, pl.ds(hk, 1), :] * s_new
            return carry

        lax.fori_loop(0, dk, body, 0)

    run(0, a_gla.T, gqk[:, dk:].T, (gqk[:, :dk] * (REC_DK ** -0.5)).T, col(C_GV).T,
        sg_in_ref, sg_out_ref, og_ref)
    run(1, rdec_ref[...], rk.T, rq.T, col(C_RV).T, sr_in_ref, sr_out_ref, or_ref)

    outs = []
    for acc_ref, g_ref, gate, center in ((og_ref, gn_ref, col(C_GR), False), (or_ref, rn_ref, col(C_RG), True)):
        heads = []
        for h in range(N_HEAD):
            o_h = acc_ref[h * REC_DV:(h + 1) * REC_DV, :]
            if center:
                o_h = o_h - jnp.mean(o_h, axis=0, keepdims=True)
            heads.append(o_h * lax.rsqrt(jnp.mean(o_h * o_h, axis=0, keepdims=True) + EPS))
        normed = jnp.concatenate(heads, axis=0) * g_ref[...]
        outs.append(normed.T * _silu(gate))
    o_ref[...] = jnp.concatenate(outs, axis=1).astype(BF16)


def _rec_sample(proj, cos, sin, lw, s_gla, s_ret):
    bd = proj.shape[0]
    dk, dv = N_HEAD * REC_DK, N_HEAD * REC_DV
    to_lanes = lambda s: s.reshape(bd, dk * REC_DV).T
    rdec = jnp.broadcast_to(jnp.repeat(_ret_tables()[4], REC_DK)[:, None], (dk, bd))
    gn_col = lw['gla_norm'].reshape(dv, 1)
    rn_col = lw['ret_norm'].reshape(dv, 1)
    st = jax.ShapeDtypeStruct((dk * REC_DV, bd), F32)
    o, sg, sr = pl.pallas_call(
        _rec_sample_kernel,
        out_shape=[jax.ShapeDtypeStruct((bd, 2 * BRANCH_W), BF16), st, st],
        scratch_shapes=[pltpu.VMEM((2, dk, bd), F32), pltpu.VMEM((2, dk, bd), F32),
                        pltpu.VMEM((2, dk, bd), F32), pltpu.VMEM((2, dv, bd), F32),
                        pltpu.VMEM((dv, bd), F32), pltpu.VMEM((dv, bd), F32)],
        compiler_params=pltpu.CompilerParams(vmem_limit_bytes=VMEM_LIMIT),
        name="rec_sample",
    )(proj, cos, sin, lw['w2'], lw['gla_b'], gn_col, rn_col, rdec, to_lanes(s_gla), to_lanes(s_ret))
    from_lanes = lambda s: s.T.reshape(bd, N_HEAD, REC_DK, REC_DV)
    return o, from_lanes(sg), from_lanes(sr)


def _prep_kernel(mcq_ref, mckv_ref, tail_ref, sq_ref, sk_ref, sv_ref, cos_ref, sin_ref,
                 nq_ref, nkv_ref, wuq_ref, wuk_ref,
                 qcat_ref, kcat_ref, ckv_ref, krope_ref, sqb_ref, skb_ref, svb_ref):
    cos, sin = cos_ref[...], sin_ref[...]
    c_q = _rms(mcq_ref[...], nq_ref[...]).astype(BF16)
    q_c = _dot(c_q, wuq_ref[...])
    q_lat = _dot(q_c[:, :N_HEAD * MLA_NOPE].astype(BF16), wuk_ref[...])
    q_rope = _rope_lanes(q_c[:, N_HEAD * MLA_NOPE:], cos, sin)
    c_kv = _rms(mckv_ref[...], nkv_ref[...])
    k_rope = _rope_lanes(tail_ref[...], cos, sin)
    lane = _iota(k_rope.shape, 1)
    ckv_ref[...] = c_kv
    krope_ref[...] = k_rope[:, :MLA_DR]
    kcat_ref[:, :MLA_DC] = c_kv.astype(BF16)
    kcat_ref[:, MLA_DC:] = jnp.where(lane < MLA_DR, k_rope, 0.0).astype(BF16)
    for h in range(N_HEAD):
        qcat_ref[h, :, :MLA_DC] = q_lat[:, h * MLA_DC:(h + 1) * MLA_DC].astype(BF16)
        shifted = q_rope if h == 0 else pltpu.roll(q_rope, LANE - h * MLA_DR, 1)
        qcat_ref[h, :, MLA_DC:] = jnp.where(lane < MLA_DR, shifted, 0.0).astype(BF16)
    sqb_ref[...] = (sq_ref[...] * (SB_DH ** -0.5)).astype(BF16)
    skb_ref[...] = sk_ref[...].astype(BF16)
    svb_ref[...] = sv_ref[...].astype(BF16)


def _prep(proj, cos, sin, lw):
    m = proj.shape[0]
    tm = min(m, 512)
    blk256 = lambda j: pl.BlockSpec((tm, 256), lambda i, j=j: (i, j))
    blk128 = lambda j: pl.BlockSpec((tm, LANE), lambda i, j=j: (i, j))
    row = lambda w: pl.BlockSpec((tm, w), lambda i: (i, 0))
    full = lambda a: pl.BlockSpec(a.shape, lambda i: (0,) * a.ndim)
    consts = (lw['mla_norm_q'], lw['mla_norm_kv'], lw['w_uq'], lw['w_uk'])
    return pl.pallas_call(
        _prep_kernel,
        grid=(m // tm,),
        in_specs=[blk256(C_MCQ), blk128(C_MCKV), blk128(C_TAIL), blk256(C_SQ), blk256(C_SK), blk256(C_SV),
                  row(LANE), row(LANE)] + [full(a) for a in consts],
        out_specs=[pl.BlockSpec((N_HEAD, tm, 256), lambda i: (0, i, 0)), row(256), row(MLA_DC), row(MLA_DR),
                   row(256), row(256), row(256)],
        out_shape=[jax.ShapeDtypeStruct((N_HEAD, m, 256), BF16), jax.ShapeDtypeStruct((m, 256), BF16),
                   jax.ShapeDtypeStruct((m, MLA_DC), F32), jax.ShapeDtypeStruct((m, MLA_DR), F32),
                   jax.ShapeDtypeStruct((m, 256), BF16), jax.ShapeDtypeStruct((m, 256), BF16),
                   jax.ShapeDtypeStruct((m, 256), BF16)],
        compiler_params=_cparams(("arbitrary",)),
        name="mixer_prep",
    )(proj, proj, proj, proj, proj, proj, cos, sin, *consts)


_EXP2_SCALE = MLA_SCALE * math.log2(math.e)


def _mla_prompt_kernel(q_ref, k_ref, wuv_ref, o_ref, m_ref, l_ref, acc_ref, *, tq, tk):
    i = pl.program_id(0)
    rows = N_HEAD * tq
    q = q_ref[...].reshape(rows, 256)
    m_ref[...] = jnp.full_like(m_ref, -jnp.inf)
    l_ref[...] = jnp.zeros_like(l_ref)
    acc_ref[...] = jnp.zeros_like(acc_ref)

    def tile(j, masked):
        k = k_ref[pl.ds(pl.multiple_of(j * tk, tk), tk), :]
        s = _dot_nt(q, k)
        if masked:
            qpos = i * tq + _iota((rows, tk), 0) % tq
            kpos = j * tk + _iota((rows, tk), 1)
            s = jnp.where(kpos <= qpos, s, -jnp.inf)
        m_old = m_ref[...]
        m_new = jnp.maximum(m_old, jnp.max(s, axis=-1, keepdims=True))
        alpha = jnp.exp2((m_old - m_new) * _EXP2_SCALE)
        p = jnp.exp2((s - m_new) * _EXP2_SCALE)
        l_ref[...] = alpha * l_ref[...] + jnp.sum(p, axis=-1, keepdims=True)
        acc_ref[...] = alpha * acc_ref[...] + _dot(p.astype(BF16), k[:, :MLA_DC])
        m_ref[...] = m_new

    n_full = (i * tq) // tk

    def body(j, carry):
        tile(j, False)
        return carry

    lax.fori_loop(0, n_full, body, 0)
    tile(n_full, True)
    o_lat = acc_ref[...] / l_ref[...]
    out = None
    for h in range(N_HEAD):
        term = _dot(o_lat[h * tq:(h + 1) * tq].astype(BF16), wuv_ref[h])
        out = term if out is None else out + term
    o_ref[...] = out.astype(BF16)


def _mla_prompt(qcat, kcat, wuv):
    l = kcat.shape[0]
    tq = min(l, 256)
    tk = min(l, 512)
    rows = N_HEAD * tq
    return pl.pallas_call(
        functools.partial(_mla_prompt_kernel, tq=tq, tk=tk),
        grid=(l // tq,),
        in_specs=[pl.BlockSpec((N_HEAD, tq, 256), lambda i: (0, i, 0)),
                  pl.BlockSpec((l, 256), lambda i: (0, 0)),
                  pl.BlockSpec(wuv.shape, lambda i: (0, 0, 0))],
        out_specs=pl.BlockSpec((tq, BRANCH_W), lambda i: (i, 0)),
        out_shape=jax.ShapeDtypeStruct((l, BRANCH_W), BF16),
        scratch_shapes=[pltpu.VMEM((rows, 1), F32), pltpu.VMEM((rows, 1), F32), pltpu.VMEM((rows, MLA_DC), F32)],
        compiler_params=_cparams(("arbitrary",)),
        name="mla_prompt",
    )(qcat, kcat, wuv)


def _strict_suffix_matrix(n):
    return (_iota((n, n), 0) > _iota((n, n), 1)).astype(BF16)


def _sb_block(q_h, k_h, v_h, carry, mask, suffix):
    z = _dot_nt(q_h, k_h)
    lsig, l1m = _log_sigmoid_pair(z)
    if mask is not None:
        l1m = jnp.where(mask, l1m, 0.0)
    cum = _dot_exact_rhs(l1m, suffix)
    a = jnp.exp(lsig + cum + carry)
    if mask is not None:
        a = jnp.where(mask, a, 0.0)
    return _dot(a.astype(BF16), v_h), carry + jnp.sum(l1m, axis=-1, keepdims=True)


def _sb_prompt_kernel(q_ref, k_ref, v_ref, o_ref, *, tq):
    i = pl.program_id(0)
    suffix = _strict_suffix_matrix(tq)
    q = q_ref[...]
    qpos = i * tq + _iota((tq, tq), 0)

    def cond(state):
        return state[0]

    def body(state):
        _, j, carries, accs = state
        rows = pl.ds(pl.multiple_of(j * tq, tq), tq)
        mask = (j * tq + _iota((tq, tq), 1)) < qpos
        new_c, new_a, alive = [], [], None
        for h in range(N_HEAD):
            hs = slice(h * SB_DH, (h + 1) * SB_DH)
            pv, c_h = _sb_block(q[:, hs], k_ref[rows, hs], v_ref[rows, hs], carries[h], mask, suffix)
            new_c.append(c_h)
            new_a.append(accs[h] + pv)
            top = jnp.max(c_h)
            alive = top if alive is None else jnp.maximum(alive, top)
        go = jnp.logical_and(j > 0, alive > SB_DEAD)
        return go, j - 1, tuple(new_c), tuple(new_a)

    init = (i >= 0, i, tuple(jnp.zeros((tq, 1), F32) for _ in range(N_HEAD)),
            tuple(jnp.zeros((tq, SB_DH), F32) for _ in range(N_HEAD)))
    _, _, _, accs = lax.while_loop(cond, body, init)
    o_ref[...] = jnp.concatenate(accs, axis=1).astype(BF16)


def _sb_prompt(sq, sk, sv):
    l = sq.shape[0]
    tq = min(l, 128)
    full = pl.BlockSpec((l, 256), lambda i: (0, 0))
    return pl.pallas_call(
        functools.partial(_sb_prompt_kernel, tq=tq),
        grid=(l // tq,),
        in_specs=[pl.BlockSpec((tq, 256), lambda i: (i, 0)), full, full],
        out_specs=pl.BlockSpec((tq, BRANCH_W), lambda i: (i, 0)),
        out_shape=jax.ShapeDtypeStruct((l, BRANCH_W), BF16),
        compiler_params=_cparams(("arbitrary",)),
        name="sb_prompt",
    )(sq, sk, sv)


def _mla_decode_kernel(pt_ref, q_ref, kself_ref, wuv_ref, ckv_hbm, kr_hbm, o_ref,
                       ckv_buf, kr_buf, sem, *, layer, n_pages, chunk, page):
    s = pl.program_id(0)
    n_seq = pl.num_programs(0)
    n_chunk = n_pages // chunk
    total = n_seq * n_chunk

    def copies(g, slot):
        seq, c = g // n_chunk, g % n_chunk
        out = []
        for p in range(chunk):
            pid = pt_ref[seq, c * chunk + p]
            dst = pl.ds(p * page, page)
            out.append(pltpu.make_async_copy(ckv_hbm.at[layer, pid], ckv_buf.at[slot, dst], sem.at[0, slot]))
            out.append(pltpu.make_async_copy(kr_hbm.at[layer, pid], kr_buf.at[slot, dst], sem.at[1, slot]))
        return out

    @pl.when(s == 0)
    def _():
        for cp in copies(0, 0):
            cp.start()

    q = q_ref[0]
    q_lat, q_rope = q[:, :MLA_DC], q[:, MLA_DC:MLA_DC + MLA_DR]
    nq = q.shape[0]

    def step(c, state):
        m_old, l_old, acc = state
        g = s * n_chunk + c
        slot = g % 2

        @pl.when(g + 1 < total)
        def _():
            for cp in copies(g + 1, 1 - slot):
                cp.start()

        for cp in copies(g, slot):
            cp.wait()
        kc = ckv_buf[slot].astype(BF16)
        sc = (_dot_nt(q_lat, kc) + _dot_nt(q_rope, kr_buf[slot].astype(BF16))) * MLA_SCALE
        m_new = jnp.maximum(m_old, jnp.max(sc, axis=-1, keepdims=True))
        alpha = jnp.exp(m_old - m_new)
        p = jnp.exp(sc - m_new)
        return (m_new, alpha * l_old + jnp.sum(p, axis=-1, keepdims=True),
                alpha * acc + _dot(p.astype(BF16), kc))

    init = (jnp.full((nq, 1), -jnp.inf, F32), jnp.zeros((nq, 1), F32), jnp.zeros((nq, MLA_DC), F32))
    m_old, l_old, acc = lax.fori_loop(0, n_chunk, step, init)
    kself = kself_ref[0].astype(F32)
    s_self = jnp.sum(q.astype(F32) * kself, axis=-1, keepdims=True) * MLA_SCALE
    m_new = jnp.maximum(m_old, s_self)
    alpha = jnp.exp(m_old - m_new)
    p_self = jnp.exp(s_self - m_new).astype(BF16).astype(F32)
    l_new = alpha * l_old + p_self
    o_lat = ((alpha * acc + p_self * kself[:, :MLA_DC]) / l_new).astype(BF16)
    out = None
    for h in range(N_HEAD):
        term = jnp.where(_iota((nq, BRANCH_W), 0) == h, _dot(o_lat, wuv_ref[h]), 0.0)
        out = term if out is None else out + term
    o_ref[0] = jnp.sum(out, axis=0, keepdims=True).astype(BF16)


def _mla_decode(page_table, q16, kself, wuv, cache_ckv, cache_kr, layer):
    bd, n_pages = page_table.shape
    page = cache_ckv.shape[2]
    chunk = math.gcd(n_pages, 8)
    grid_spec = pltpu.PrefetchScalarGridSpec(
        num_scalar_prefetch=1,
        grid=(bd,),
        in_specs=[pl.BlockSpec((1,) + q16.shape[1:], lambda s, pt: (s, 0, 0)),
                  pl.BlockSpec((1, 1, 256), lambda s, pt: (s, 0, 0)),
                  pl.BlockSpec(wuv.shape, lambda s, pt: (0, 0, 0)),
                  pl.BlockSpec(memory_space=pl.ANY),
                  pl.BlockSpec(memory_space=pl.ANY)],
        out_specs=pl.BlockSpec((1, 1, BRANCH_W), lambda s, pt: (s, 0, 0)),
        scratch_shapes=[pltpu.VMEM((2, chunk * page, MLA_DC), F32),
                        pltpu.VMEM((2, chunk * page, MLA_DR), F32),
                        pltpu.SemaphoreType.DMA((2, 2))],
    )
    out = pl.pallas_call(
        functools.partial(_mla_decode_kernel, layer=layer, n_pages=n_pages, chunk=chunk, page=page),
        grid_spec=grid_spec,
        out_shape=jax.ShapeDtypeStruct((bd, 1, BRANCH_W), BF16),
        compiler_params=_cparams(("arbitrary",)),
        name="mla_decode",
    )(page_table, q16, kself, wuv, cache_ckv, cache_kr)
    return out.reshape(bd, BRANCH_W)


def _sb_decode_kernel(pt_ref, q_ref, k_hbm, v_hbm, o_ref, kbuf, vbuf, kx, vx, sem, semx,
                      *, layer, n_pages, group, page):
    blk = pl.program_id(0)
    suffix = _strict_suffix_matrix(page)

    def head_copies(g, j):
        pid = pt_ref[blk * group + g, n_pages - 1 - j]
        return (pltpu.make_async_copy(k_hbm.at[layer, pid], kbuf.at[g, j], sem.at[0, g, j]),
                pltpu.make_async_copy(v_hbm.at[layer, pid], vbuf.at[g, j], sem.at[1, g, j]))

    for g in range(group):
        for j in range(2):
            for cp in head_copies(g, j):
                cp.start()

    def page_step(q8, kref, vref, carry, accs):
        z = None
        for h in range(N_HEAD):
            term = _dot_nt(q8[h], kref[:, h, :].astype(BF16))
            z = term if z is None else z + term
        lsig, l1m = _log_sigmoid_pair(z)
        cum = _dot_exact_rhs(l1m, suffix)
        a = jnp.exp(lsig + cum + carry).astype(BF16)
        new_accs = tuple(accs[h] + _dot(a, vref[:, h, :].astype(BF16)) for h in range(N_HEAD))
        return carry + jnp.sum(l1m, axis=-1, keepdims=True), new_accs

    rows8 = _iota((8, SB_DH), 0)
    for g in range(group):
        seq = blk * group + g
        qrow = q_ref[g].astype(F32)
        q8 = [jnp.where(rows8 == h, jnp.broadcast_to(qrow[:, h * SB_DH:(h + 1) * SB_DH], (8, SB_DH)),
                        0.0).astype(BF16) for h in range(N_HEAD)]
        carry = jnp.zeros((8, 1), F32)
        accs = tuple(jnp.zeros((8, SB_DH), F32) for _ in range(N_HEAD))
        for cp in head_copies(g, 0):
            cp.wait()
        carry, accs = page_step(q8, kbuf.at[g, 0], vbuf.at[g, 0], carry, accs)
        for cp in head_copies(g, 1):
            cp.wait()
        carry, accs = page_step(q8, kbuf.at[g, 1], vbuf.at[g, 1], carry, accs)

        def cond(state):
            return state[0]

        def body(state):
            _, j, c, a = state
            pid = pt_ref[seq, n_pages - 1 - j]
            ck = pltpu.make_async_copy(k_hbm.at[layer, pid], kx, semx.at[0])
            cv = pltpu.make_async_copy(v_hbm.at[layer, pid], vx, semx.at[1])
            ck.start()
            cv.start()
            ck.wait()
            cv.wait()
            c, a = page_step(q8, kx, vx, c, a)
            go = jnp.logical_and(j + 1 < n_pages, jnp.max(c[:N_HEAD]) > SB_DEAD)
            return go, j + 1, c, a

        go0 = jnp.logical_and(n_pages > 2, jnp.max(carry[:N_HEAD]) > SB_DEAD)
        _, _, carry, accs = lax.while_loop(cond, body, (go0, jnp.int32(2), carry, accs))
        out = jnp.concatenate(accs, axis=1)
        keep = _iota(out.shape, 0) == _iota(out.shape, 1) // SB_DH
        o_ref[g] = jnp.sum(jnp.where(keep, out, 0.0), axis=0, keepdims=True).astype(BF16)


def _sb_decode(page_table, sq, cache_k, cache_v, layer):
    bd, n_pages = page_table.shape
    assert n_pages >= 2
    page = cache_k.shape[2]
    group = math.gcd(bd, 8)
    slab = (page, N_HEAD, SB_DH)
    grid_spec = pltpu.PrefetchScalarGridSpec(
        num_scalar_prefetch=1,
        grid=(bd // group,),
        in_specs=[pl.BlockSpec((group, 1, 256), lambda b, pt: (b, 0, 0)),
                  pl.BlockSpec(memory_space=pl.ANY),
                  pl.BlockSpec(memory_space=pl.ANY)],
        out_specs=pl.BlockSpec((group, 1, BRANCH_W), lambda b, pt: (b, 0, 0)),
        scratch_shapes=[pltpu.VMEM((group, 2) + slab, F32), pltpu.VMEM((group, 2) + slab, F32),
                        pltpu.VMEM(slab, F32), pltpu.VMEM(slab, F32),
                        pltpu.SemaphoreType.DMA((2, group, 2)), pltpu.SemaphoreType.DMA((2,))],
    )
    out = pl.pallas_call(
        functools.partial(_sb_decode_kernel, layer=layer, n_pages=n_pages, group=group, page=page),
        grid_spec=grid_spec,
        out_shape=jax.ShapeDtypeStruct((bd, 1, BRANCH_W), BF16),
        compiler_params=_cparams(("arbitrary",)),
        name="sb_decode",
    )(page_table, sq.reshape(bd, 1, 256), cache_k, cache_v)
    return out.reshape(bd, BRANCH_W)


def _merge_kernel(x_ref, oab_ref, oc_ref, od_ref, gpre_ref, gpost_ref, wg_ref, wb_ref, wo_ref, y_ref):
    x = x_ref[...]
    h = _rms(x, gpre_ref[...]).astype(BF16)
    oab = oab_ref[...]
    branches = (oab[:, :BRANCH_W], oab[:, BRANCH_W:], oc_ref[...], od_ref[...])
    d = x.shape[1]
    merged = None
    for n in range(N_BRANCH):
        gate = jax.nn.sigmoid(_dot(h, wg_ref[:, n * d:(n + 1) * d]))
        term = gate * _dot(branches[n], wb_ref[n])
        merged = term if merged is None else merged + term
    mix = _dot(merged.astype(BF16), wo_ref[...])
    y_ref[...] = x + _rms(mix, gpost_ref[...])


def _merge(x, oab, oc, od, lw):
    m, d = x.shape
    tm = min(m, 512)
    row = lambda w: pl.BlockSpec((tm, w), lambda i: (i, 0))
    full = lambda a: pl.BlockSpec(a.shape, lambda i: (0,) * a.ndim)
    consts = (lw['norm_mix_pre'], lw['norm_mix_post'], lw['w_gate'], lw['w_branch'], lw['w_out'])
    return pl.pallas_call(
        _merge_kernel,
        grid=(m // tm,),
        in_specs=[row(d), row(2 * BRANCH_W), row(BRANCH_W), row(BRANCH_W)] + [full(a) for a in consts],
        out_specs=row(d),
        out_shape=jax.ShapeDtypeStruct((m, d), F32),
        compiler_params=_cparams(("arbitrary",)),
        name="merge",
    )(x, oab, oc, od, *consts)


def _gelu(x):
    return 0.5 * x * (1.0 + lax.erf(x * (2.0 ** -0.5)))


def _ffn_tail(c, x_ref, gpost_ref, acc_ref, y_ref):
    @pl.when(c == pl.num_programs(1) - 1)
    def _():
        y_ref[...] = x_ref[...] + _rms(acc_ref[...], gpost_ref[...])


def _ffn_prompt_kernel(x_ref, gpre_ref, gpost_ref, wg_ref, wu_ref, cw_ref, cb_ref, wo_ref,
                       y_ref, tail_ref, h_ref, acc_ref, prev_ref):
    i, c = pl.program_id(0), pl.program_id(1)

    @pl.when(c == 0)
    def _():
        h_ref[...] = _rms(x_ref[...], gpre_ref[...]).astype(BF16)
        acc_ref[...] = jnp.zeros_like(acc_ref)

    @pl.when(i == 0)
    def _():
        prev_ref[c] = jnp.zeros(prev_ref.shape[1:], F32)

    h = h_ref[...]
    gp = _dot(h, wg_ref[...])
    tm = gp.shape[0]
    prev = prev_ref[c]
    row = _iota(gp.shape, 0)
    back1 = jnp.where(row == 0, prev[7:8], pltpu.roll(gp, 1, 0))
    back2 = jnp.where(row == 0, prev[6:7], jnp.where(row == 1, prev[7:8], pltpu.roll(gp, 2, 0)))
    cw = cw_ref[...]
    conv = cb_ref[...] + cw[0:1] * back2 + cw[1:2] * back1 + cw[2:3] * gp
    act = _gelu(conv) * _dot(h, wu_ref[...])
    acc_ref[...] += _dot(act.astype(BF16), wo_ref[...])
    prev_ref[c] = gp[tm - 8:tm]
    tail_ref[0] = gp[tm - (CONV_W - 1):tm]
    _ffn_tail(c, x_ref, gpost_ref, acc_ref, y_ref)


def _ffn_sample_kernel(x_ref, gpre_ref, gpost_ref, wg_ref, wu_ref, cw_ref, cb_ref, wo_ref, s0_ref, s1_ref,
                       y_ref, gp_ref, h_ref, acc_ref):
    c = pl.program_id(1)

    @pl.when(c == 0)
    def _():
        h_ref[...] = _rms(x_ref[...], gpre_ref[...]).astype(BF16)
        acc_ref[...] = jnp.zeros_like(acc_ref)

    h = h_ref[...]
    gp = _dot(h, wg_ref[...])
    cw = cw_ref[...]
    conv = cb_ref[...] + cw[0:1] * s0_ref[...] + cw[1:2] * s1_ref[...] + cw[2:3] * gp
    act = _gelu(conv) * _dot(h, wu_ref[...])
    acc_ref[...] += _dot(act.astype(BF16), wo_ref[...])
    gp_ref[...] = gp
    _ffn_tail(c, x_ref, gpost_ref, acc_ref, y_ref)


def _ffn(x, lw, state=None):
    m, d = x.shape
    dff = lw['w_ffn_out'].shape[0]
    nck = 2
    ck = dff // nck
    tm = min(m, 512)
    xrow = pl.BlockSpec((tm, d), lambda i, c: (i, 0))
    vec = lambda a: pl.BlockSpec(a.shape, lambda i, c: (0,) * a.ndim)
    in_specs = [xrow, vec(lw['norm_ffn_pre']), vec(lw['norm_ffn_post']),
                pl.BlockSpec((d, ck), lambda i, c: (0, c)),
                pl.BlockSpec((d, ck), lambda i, c: (0, c + nck)),
                pl.BlockSpec((CONV_W, ck), lambda i, c: (0, c)),
                pl.BlockSpec((1, ck), lambda i, c: (0, c)),
                pl.BlockSpec((ck, d), lambda i, c: (c, 0))]
    args = [x, lw['norm_ffn_pre'], lw['norm_ffn_post'], lw['w_ffn_in'], lw['w_ffn_in'],
            lw['ffn_conv_w'], lw['ffn_conv_b'], lw['w_ffn_out']]
    scratch = [pltpu.VMEM((tm, d), BF16), pltpu.VMEM((tm, d), F32)]
    if state is None:
        kern = _ffn_prompt_kernel
        out_specs = [xrow, pl.BlockSpec((1, CONV_W - 1, ck), lambda i, c: (i, 0, c))]
        out_shape = [jax.ShapeDtypeStruct((m, d), F32), jax.ShapeDtypeStruct((m // tm, CONV_W - 1, dff), F32)]
        scratch = scratch + [pltpu.VMEM((nck, 8, ck), F32)]
    else:
        kern = _ffn_sample_kernel
        srow = pl.BlockSpec((tm, ck), lambda i, c: (i, c))
        in_specs += [srow, srow]
        args += [state[:, 0, :], state[:, 1, :]]
        out_specs = [xrow, srow]
        out_shape = [jax.ShapeDtypeStruct((m, d), F32), jax.ShapeDtypeStruct((m, dff), F32)]
    return pl.pallas_call(
        kern,
        grid=(m // tm, nck),
        in_specs=in_specs,
        out_specs=out_specs,
        out_shape=out_shape,
        scratch_shapes=scratch,
        compiler_params=_cparams(("arbitrary", "arbitrary")),
        name="conv_ffn",
    )(*args)


def _rope_tables(pos):
    half = MLA_DR // 2
    inv_freq = jnp.exp(-math.log(ROPE_BASE) * jnp.arange(half, dtype=F32) / half)
    ang = pos.astype(F32)[:, None] * inv_freq[None, :]
    cos, sin = jnp.cos(ang), jnp.sin(ang)
    reps = LANE // MLA_DR
    return (jnp.tile(jnp.concatenate([cos, cos], axis=1), (1, reps)),
            jnp.tile(jnp.concatenate([-sin, sin], axis=1), (1, reps)))


def _layer_weights(l, w_in, gla_w_gate2, gla_b_gate, gla_norm, ret_norm, mla_norm_q, mla_norm_kv, mla_w_uq,
                   mla_w_uk, mla_w_uv, w_branch, w_out, w_ffn_in, ffn_conv_w, ffn_conv_b, w_ffn_out,
                   norm_mix_pre, norm_mix_post, norm_ffn_pre, norm_ffn_post):
    d = w_in.shape[1]
    hk, hv = N_HEAD * REC_DK, N_HEAD * REC_DV
    sizes = (hk, hk, hv, GLA_RANK, hv, hk, hk, hv, hv, MLA_DQ, MLA_DC, MLA_DR,
             N_HEAD * SB_DH, N_HEAD * SB_DH, N_HEAD * SB_DH, N_BRANCH * d)
    pts, acc = [], 0
    for s in sizes[:-1]:
        acc += s
        pts.append(acc)
    (g_q, g_k, g_v, g_lr, g_r, r_q, r_k, r_v, r_g, m_cq, m_ckv, m_kr, s_q, s_k, s_v, w_gate) = jnp.split(
        w_in[l], pts, axis=1)
    pad = jnp.zeros((d, LANE - MLA_DR - GLA_RANK), F32)
    w_mix = jnp.concatenate([g_q, g_k, g_v, g_r, r_q, r_k, r_v, r_g, m_cq, s_q, s_k, s_v, m_ckv,
                             m_kr, g_lr, pad], axis=1)
    assert w_mix.shape[1] == D_MIX
    w2 = jnp.zeros((LANE, hk), F32).at[TAIL_GLR:TAIL_GLR + GLA_RANK].set(gla_w_gate2[l])
    uq = mla_w_uq[l].reshape(MLA_DQ, N_HEAD, MLA_NOPE + MLA_DR)
    w_uq = jnp.concatenate([uq[:, :, :MLA_NOPE].reshape(MLA_DQ, -1), uq[:, :, MLA_NOPE:].reshape(MLA_DQ, -1)], axis=1)
    uk = jnp.transpose(mla_w_uk[l], (1, 2, 0))
    w_uk = jnp.zeros((N_HEAD, MLA_NOPE, N_HEAD, MLA_DC), F32)
    for h in range(N_HEAD):
        w_uk = w_uk.at[h, :, h, :].set(uk[h])
    w_uk = w_uk.reshape(N_HEAD * MLA_NOPE, N_HEAD * MLA_DC)
    uv = jnp.transpose(mla_w_uv[l], (1, 0, 2))
    w_uv = jnp.zeros((N_HEAD, MLA_DC, N_HEAD, MLA_DV), F32)
    for h in range(N_HEAD):
        w_uv = w_uv.at[h, :, h, :].set(uv[h])
    w_uv = w_uv.reshape(N_HEAD, MLA_DC, N_HEAD * MLA_DV)
    row = lambda a: a[l].reshape(1, -1)
    return {
        'w_mix': w_mix.astype(BF16), 'w_gate': w_gate.astype(BF16), 'w2': w2.astype(BF16),
        'gla_b': row(gla_b_gate), 'gla_norm': row(gla_norm), 'ret_norm': row(ret_norm),
        'mla_norm_q': row(mla_norm_q), 'mla_norm_kv': row(mla_norm_kv),
        'w_uq': w_uq.astype(BF16), 'w_uk': w_uk.astype(BF16), 'w_uv': w_uv.astype(BF16),
        'w_branch': w_branch[l].astype(BF16), 'w_out': w_out[l].astype(BF16),
        'w_ffn_in': w_ffn_in[l].astype(BF16), 'ffn_conv_w': ffn_conv_w[l], 'ffn_conv_b': row(ffn_conv_b),
        'w_ffn_out': w_ffn_out[l].astype(BF16),
        'norm_mix_pre': row(norm_mix_pre), 'norm_mix_post': row(norm_mix_post),
        'norm_ffn_pre': row(norm_ffn_pre), 'norm_ffn_post': row(norm_ffn_post),
    }


def kernel(x_prompt, x_sample, cache_mla_ckv, cache_mla_krope, cache_sb_k, cache_sb_v, state_gla, state_ret, state_ffn_conv, page_table, w_in, gla_w_gate2, gla_b_gate, gla_norm, ret_norm, mla_norm_q, mla_norm_kv, mla_w_uq, mla_w_uk, mla_w_uv, w_branch, w_out, w_ffn_in, ffn_conv_w, ffn_conv_b, w_ffn_out, norm_mix_pre, norm_mix_post, norm_ffn_pre, norm_ffn_post):
    bp, seq, d = x_prompt.shape
    bd, dseq, _ = x_sample.shape
    assert bp == 1 and dseq == 1
    depth = w_in.shape[0]
    past = page_table.shape[1] * cache_mla_ckv.shape[2]
    cos_p, sin_p = _rope_tables(jnp.arange(seq, dtype=jnp.int32))
    cos_s, sin_s = _rope_tables(jnp.full((bd,), past, jnp.int32))

    xp = x_prompt.reshape(seq, d)
    xs = x_sample.reshape(bd, d)
    p_rows = [[] for _ in range(7)]
    s_rows = [[] for _ in range(7)]
    for l in range(depth):
        lw = _layer_weights(l, w_in, gla_w_gate2, gla_b_gate, gla_norm, ret_norm, mla_norm_q, mla_norm_kv,
                            mla_w_uq, mla_w_uk, mla_w_uv, w_branch, w_out, w_ffn_in, ffn_conv_w, ffn_conv_b,
                            w_ffn_out, norm_mix_pre, norm_mix_post, norm_ffn_pre, norm_ffn_post)
        proj = _norm_matmul(xp, lw['norm_mix_pre'], lw['w_mix'])
        oab, gla_new, ret_new = _rec_prompt(proj, cos_p, sin_p, lw)
        qcat, kcat, ckv, krope, sqb, skb, svb = _prep(proj, cos_p, sin_p, lw)
        oc = _mla_prompt(qcat, kcat, lw['w_uv'])
        od = _sb_prompt(sqb, skb, svb)
        xp = _merge(xp, oab, oc, od, lw)
        xp, conv_tails = _ffn(xp, lw)
        conv_tail = conv_tails[-1]
        sk_cols = slice(C_SK * 256, (C_SK + 1) * 256)
        sv_cols = slice(C_SV * 256, (C_SV + 1) * 256)
        for i, a in enumerate((ckv.reshape(1, seq, MLA_DC), krope.reshape(1, seq, MLA_DR),
                               proj[:, sk_cols].reshape(1, seq, N_HEAD, SB_DH),
                               proj[:, sv_cols].reshape(1, seq, N_HEAD, SB_DH),
                               gla_new[None], ret_new[None], conv_tail[None])):
            p_rows[i].append(a)

        proj = _norm_matmul(xs, lw['norm_mix_pre'], lw['w_mix'])
        oab, gla_new, ret_new = _rec_sample(proj, cos_s, sin_s, lw, state_gla[l], state_ret[l])
        qcat, kcat, ckv, krope, sqb, skb, svb = _prep(proj, cos_s, sin_s, lw)
        q16 = jnp.zeros((bd, 16, 256), BF16).at[:, :N_HEAD].set(jnp.transpose(qcat, (1, 0, 2)))
        oc = _mla_decode(page_table, q16, kcat.reshape(bd, 1, 256), lw['w_uv'], cache_mla_ckv, cache_mla_krope, l)
        od = _sb_decode(page_table, sqb, cache_sb_k, cache_sb_v, l)
        xs = _merge(xs, oab, oc, od, lw)
        xs, gate_pre = _ffn(xs, lw, state_ffn_conv[l])
        conv_new = jnp.stack([state_ffn_conv[l][:, 1, :], gate_pre], axis=1)
        for i, a in enumerate((ckv.reshape(bd, 1, MLA_DC), krope.reshape(bd, 1, MLA_DR),
                               proj[:, sk_cols].reshape(bd, 1, N_HEAD, SB_DH),
                               proj[:, sv_cols].reshape(bd, 1, N_HEAD, SB_DH),
                               gla_new, ret_new, conv_new)):
            s_rows[i].append(a)

    outs_p = [jnp.stack(a, axis=0) for a in p_rows]
    outs_s = [jnp.stack(a, axis=0) for a in s_rows]
    return (xp.reshape(1, seq, d), xs.reshape(bd, 1, d), *outs_p, *outs_s)
```

```python
import functools
import math

import jax
import jax.numpy as jnp
from jax import lax
from jax.experimental import pallas as pl
from jax.experimental.pallas import tpu as pltpu

F32 = jnp.float32
BF16 = jnp.bfloat16

N_HEAD = 4
REC_DK = 32
REC_DV = 64
GLA_RANK = 16
GLA_TAU = 16.0
MLA_DQ = 256
MLA_DC = 128
MLA_DR = 32
MLA_NOPE = 64
MLA_DV = 64
MLA_SCALE = (MLA_NOPE + MLA_DR) ** -0.5
SB_DH = 64
N_BRANCH = 4
BRANCH_W = 256
CONV_W = 3
ROPE_BASE = 10000.0
EPS = 1e-6
REC_CHUNK = 64
SUB = 16
SB_DEAD = -104.0
MLA_DECODE_CHUNK = 32
MLA_ONE = MLA_DC + MLA_DR
SB_TQ = 256
MLA_TQ, MLA_TK = 512, 1024

LANE = 128
VMEM_LIMIT = 52 * 1024 * 1024

C_GQK, C_GV, C_GR, C_RQK, C_RV, C_RG, C_MCQ, C_SQ, C_SK, C_SV = range(10)
C_MCKV = 20
C_TAIL = 21
D_MIX = 22 * LANE
TAIL_GLR = MLA_DR


def _cparams(sem):
    return pltpu.CompilerParams(dimension_semantics=sem, vmem_limit_bytes=VMEM_LIMIT)


def _dot(a, b):
    return jnp.dot(a, b, preferred_element_type=F32)


def _dot_nt(a, b):
    return lax.dot_general(a, b, (((1,), (1,)), ((), ())), preferred_element_type=F32)


def _dot_tn(a, b):
    return lax.dot_general(a, b, (((0,), (0,)), ((), ())), preferred_element_type=F32)


def _split3(x):
    hi = x.astype(BF16)
    r1 = x - hi.astype(F32)
    mid = r1.astype(BF16)
    lo = (r1 - mid.astype(F32)).astype(BF16)
    return hi, mid, lo


def _dot_exact_rhs(x, m):
    hi, mid, lo = _split3(x)
    return _dot(hi, m) + _dot(mid, m) + _dot(lo, m)


def _dot_exact_lhs(m, x):
    hi, mid, lo = _split3(x)
    return _dot(m, hi) + _dot(m, mid) + _dot(m, lo)


def _log_sigmoid_pair(z):
    t = jnp.log1p(jnp.exp(-jnp.abs(z)))
    return jnp.minimum(z, 0.0) - t, -jnp.maximum(z, 0.0) - t


def _rms(x, g):
    return x * lax.rsqrt(jnp.mean(x * x, axis=-1, keepdims=True) + EPS) * g


def _iota(shape, dim):
    return lax.broadcasted_iota(jnp.int32, shape, dim)


def _rope_lanes(x, cos, sin_signed):
    lane = _iota(x.shape, 1)
    swapped = jnp.where((lane % 32) < 16, pltpu.roll(x, 112, 1), pltpu.roll(x, 16, 1))
    return x * cos + swapped * sin_signed


def _head_mask(shape, lane_w, h):
    lane = _iota(shape, len(shape) - 1)
    return (lane >= h * lane_w) & (lane < (h + 1) * lane_w)


def _norm_matmul_kernel(x_ref, g_ref, w_ref, o_ref, h_ref):
    @pl.when(pl.program_id(1) == 0)
    def _():
        h_ref[...] = _rms(x_ref[...], g_ref[...]).astype(BF16)

    o_ref[...] = _dot(h_ref[...], w_ref[...])


def _norm_matmul(x, g, w):
    m, d = x.shape
    n = w.shape[1]
    tm = min(m, 1024)
    tn = n // 2
    return pl.pallas_call(
        _norm_matmul_kernel,
        grid=(m // tm, n // tn),
        in_specs=[pl.BlockSpec((tm, d), lambda i, j: (i, 0)),
                  pl.BlockSpec((1, d), lambda i, j: (0, 0)),
                  pl.BlockSpec((d, tn), lambda i, j: (0, j))],
        out_specs=pl.BlockSpec((tm, tn), lambda i, j: (i, j)),
        out_shape=jax.ShapeDtypeStruct((m, n), F32),
        scratch_shapes=[pltpu.VMEM((tm, d), BF16)],
        compiler_params=_cparams(("arbitrary", "arbitrary")),
        name="norm_matmul",
    )(x, g, w)


def _block_diag_mask():
    r = _iota((N_HEAD * REC_DK, N_HEAD * REC_DV), 0) // REC_DK
    c = _iota((N_HEAD * REC_DK, N_HEAD * REC_DV), 1) // REC_DV
    return r == c


def _head_avg_matrix():
    r = _iota((N_HEAD * REC_DV, N_HEAD * REC_DV), 0) // REC_DV
    c = _iota((N_HEAD * REC_DV, N_HEAD * REC_DV), 1) // REC_DV
    return jnp.where(r == c, 1.0 / REC_DV, 0.0).astype(BF16)


def _head_rms_lanes(o, g, avg):
    ms = _dot_exact_rhs(o * o, avg)
    return o * lax.rsqrt(ms + EPS) * g


def _head_groupnorm_lanes(o, g, avg):
    c = o - _dot_exact_rhs(o, avg)
    var = _dot_exact_rhs(c * c, avg)
    return c * lax.rsqrt(var + EPS) * g


def _silu(x):
    return x * jax.nn.sigmoid(x)


def _stack_heads(x, lane_w):
    return jnp.concatenate(
        [jnp.where(_head_mask(x.shape, lane_w, h), x, 0.0) for h in range(N_HEAD)], axis=0)


def _unstack_heads(o, rows, lane_w):
    acc = None
    for h in range(N_HEAD):
        blk = o[h * rows:(h + 1) * rows, :]
        term = jnp.where(_head_mask(blk.shape, lane_w, h), blk, 0.0)
        acc = term if acc is None else acc + term
    return acc


def _gla_chunk(q, k, v, log_a, s_gla, tri, expand, bmask):
    c = REC_CHUNK
    b = _dot_exact_lhs(tri, log_a)
    vb = v.astype(BF16)
    o = _dot((q * jnp.exp(b)).astype(BF16), s_gla.astype(BF16))
    row = _iota((SUB, N_HEAD * REC_DK), 0)
    parts = []
    for blk in range(c // SUB):
        lo = blk * SUB
        q_i, b_i = q[lo:lo + SUB], b[lo:lo + SUB]
        prods = []
        for j in range(SUB):
            rel = jnp.where(row >= j, b_i - b[lo + j:lo + j + 1], -jnp.inf)
            prods.append(q_i * k[lo + j:lo + j + 1] * jnp.exp(rel))
        p_all = jnp.concatenate(prods, axis=0).astype(BF16)
        r_all = _dot(p_all, expand)
        o_i = None
        for j in range(SUB):
            term = r_all[j * SUB:(j + 1) * SUB] * v[lo + j:lo + j + 1]
            o_i = term if o_i is None else o_i + term
        if blk > 0:
            ref = b[lo - 1:lo]
            q_t = q_i * jnp.exp(b_i - ref)
            k_rows = _iota((c, N_HEAD * REC_DK), 0)
            k_t = jnp.where(k_rows < lo, k * jnp.exp(jnp.minimum(ref - b, 0.0)), 0.0)
            s = _dot_nt(_stack_heads(q_t, REC_DK).astype(BF16), k_t.astype(BF16))
            o_i = o_i + _unstack_heads(_dot(s.astype(BF16), vb), SUB, REC_DV)
        parts.append(o_i)
    o = o + jnp.concatenate(parts, axis=0)
    b_end = b[c - 1:c]
    decay_rows = jnp.broadcast_to(jnp.exp(b_end), (LANE, LANE)).T
    decay_rows = jnp.concatenate([decay_rows, decay_rows], axis=1)
    upd = _dot_tn((k * jnp.exp(b_end - b)).astype(BF16), vb)
    s_new = decay_rows * s_gla + jnp.where(bmask, upd, 0.0)
    return o, s_new


def _ret_chunk(q, k, v, s_ret, dmat, qdec, kdec, cdec, bmask):
    c = REC_CHUNK
    vb = v.astype(BF16)
    qb = q.astype(BF16)
    o_inter = _dot(qb, s_ret.astype(BF16)) * qdec
    s = _dot_nt(_stack_heads(q, REC_DK).astype(BF16), k.astype(BF16)) * dmat
    o_intra = _unstack_heads(_dot(s.astype(BF16), vb), c, REC_DV)
    upd = _dot_tn((k * kdec).astype(BF16), vb)
    s_new = cdec * s_ret + jnp.where(bmask, upd, 0.0)
    return o_inter + o_intra, s_new


def _rec_prompt_kernel(gqk_ref, gv_ref, gr_ref, rqk_ref, rv_ref, rg_ref, tail_ref, cos_ref, sin_ref,
                       w2_ref, bg_ref, gn_ref, rn_ref, dmat_ref, qdec_ref, kdec_ref, cdec_ref,
                       o_ref, sg_out_ref, sr_out_ref, sg_ref, sr_ref, *, tc):
    step = pl.program_id(0)

    @pl.when(step == 0)
    def _():
        sg_ref[...] = jnp.zeros_like(sg_ref)
        sr_ref[...] = jnp.zeros_like(sr_ref)

    c = REC_CHUNK
    tri = (_iota((c, c), 0) >= _iota((c, c), 1)).astype(BF16)
    bmask = _block_diag_mask()
    expand = bmask.astype(BF16)
    avg = _head_avg_matrix()
    dk = N_HEAD * REC_DK

    def chunk(ci, carry):
        rows = pl.ds(pl.multiple_of(ci * c, c), c)
        gqk = gqk_ref[rows, :]
        x = _dot(tail_ref[rows, :].astype(BF16), w2_ref[...]) + bg_ref[...]
        log_a = _log_sigmoid_pair(x)[0] * (1.0 / GLA_TAU)
        o_g, sg_new = _gla_chunk(gqk[:, :dk] * (REC_DK ** -0.5), gqk[:, dk:], gv_ref[rows, :], log_a,
                                 sg_ref[...], tri, expand, bmask)
        sg_ref[...] = sg_new
        o_a = _head_rms_lanes(o_g, gn_ref[...], avg) * _silu(gr_ref[rows, :])

        rqk = rqk_ref[rows, :]
        cos, sin = cos_ref[rows, :], sin_ref[rows, :]
        rq = _rope_lanes(rqk[:, :dk], cos, sin)
        rk = _rope_lanes(rqk[:, dk:], cos, sin) * (REC_DK ** -0.5)
        o_r, sr_new = _ret_chunk(rq, rk, rv_ref[rows, :], sr_ref[...], dmat_ref[...], qdec_ref[...],
                                 kdec_ref[...], cdec_ref[...], bmask)
        sr_ref[...] = sr_new
        o_b = _head_groupnorm_lanes(o_r, rn_ref[...], avg) * _silu(rg_ref[rows, :])
        o_ref[rows, :] = jnp.concatenate([o_a, o_b], axis=1).astype(BF16)
        return carry

    lax.fori_loop(0, tc // c, chunk, 0)

    @pl.when(step == pl.num_programs(0) - 1)
    def _():
        for h in range(N_HEAD):
            sg_out_ref[h] = sg_ref[h * REC_DK:(h + 1) * REC_DK, h * REC_DV:(h + 1) * REC_DV]
            sr_out_ref[h] = sr_ref[h * REC_DK:(h + 1) * REC_DK, h * REC_DV:(h + 1) * REC_DV]


def _ret_tables():
    c = REC_CHUNK
    log_g = jnp.log1p(-jnp.exp2(-5.0 - jnp.arange(N_HEAD, dtype=F32)))
    idx = jnp.arange(c, dtype=F32)
    rel = idx[:, None] - idx[None, :]
    decay = jnp.exp(jnp.where(rel[None] >= 0, rel[None] * log_g[:, None, None], -jnp.inf))
    q_dec = jnp.exp((idx[:, None] + 1.0) * log_g[None, :])
    k_dec = jnp.exp((c - 1.0 - idx[:, None]) * log_g[None, :])
    chunk_dec = jnp.exp(c * log_g)
    dmat = decay.reshape(N_HEAD * c, c)
    qdec = jnp.repeat(q_dec, REC_DV, axis=1)
    kdec = jnp.repeat(k_dec, REC_DK, axis=1)
    cdec = jnp.broadcast_to(jnp.repeat(chunk_dec, REC_DK)[:, None], (N_HEAD * REC_DK, N_HEAD * REC_DV))
    return dmat, qdec, kdec, cdec, jnp.exp(log_g)


def _rec_prompt(proj, cos, sin, lw):
    l = proj.shape[0]
    tc = min(l, 256)
    dmat, qdec, kdec, cdec, _ = _ret_tables()
    blk256 = lambda j: pl.BlockSpec((tc, 256), lambda i, j=j: (i, j))
    blk128 = lambda j: pl.BlockSpec((tc, LANE), lambda i, j=j: (i, j))
    row128 = pl.BlockSpec((tc, LANE), lambda i: (i, 0))
    full = lambda a: pl.BlockSpec(a.shape, lambda i: (0,) * a.ndim)
    consts = (lw['w2'], lw['gla_b'], lw['gla_norm'], lw['ret_norm'], dmat, qdec, kdec, cdec)
    state = jax.ShapeDtypeStruct((N_HEAD, REC_DK, REC_DV), F32)
    return pl.pallas_call(
        functools.partial(_rec_prompt_kernel, tc=tc),
        grid=(l // tc,),
        in_specs=[blk256(C_GQK), blk256(C_GV), blk256(C_GR), blk256(C_RQK), blk256(C_RV), blk256(C_RG),
                  blk128(C_TAIL), row128, row128] + [full(a) for a in consts],
        out_specs=[pl.BlockSpec((tc, 2 * BRANCH_W), lambda i: (i, 0)),
                   pl.BlockSpec(state.shape, lambda i: (0, 0, 0)),
                   pl.BlockSpec(state.shape, lambda i: (0, 0, 0))],
        out_shape=[jax.ShapeDtypeStruct((l, 2 * BRANCH_W), BF16), state, state],
        scratch_shapes=[pltpu.VMEM((N_HEAD * REC_DK, N_HEAD * REC_DV), F32),
                        pltpu.VMEM((N_HEAD * REC_DK, N_HEAD * REC_DV), F32)],
        compiler_params=_cparams(("arbitrary",)),
        name="rec_prompt",
    )(proj, proj, proj, proj, proj, proj, proj, cos, sin, *consts)


def _rec_sample_kernel(proj_ref, cos_ref, sin_ref, w2_ref, bg_ref, gn_ref, rn_ref, rdec_ref,
                       sg_in_ref, sr_in_ref, o_ref, sg_out_ref, sr_out_ref,
                       a_ref, k_ref, q_ref, v_ref, og_ref, or_ref):
    dk = N_HEAD * REC_DK
    dv = N_HEAD * REC_DV
    col = lambda j, w=256: proj_ref[:, j * w:(j + 1) * w]
    gqk, rqk = col(C_GQK), col(C_RQK)
    x = _dot(col(C_TAIL, LANE).astype(BF16), w2_ref[...]) + bg_ref[...]
    a_gla = jnp.exp(_log_sigmoid_pair(x)[0] * (1.0 / GLA_TAU))
    cos, sin = cos_ref[...], sin_ref[...]
    rq = _rope_lanes(rqk[:, :dk], cos, sin)
    rk = _rope_lanes(rqk[:, dk:], cos, sin) * (REC_DK ** -0.5)

    def run(idx, a_t, k_t, q_t, v_t, s_in_ref, s_out_ref, oacc_ref):
        a_ref[idx], k_ref[idx], q_ref[idx], v_ref[idx] = a_t, k_t, q_t, v_t
        oacc_ref[...] = jnp.zeros_like(oacc_ref)

        def body(hk, carry):
            h = hk // REC_DK
            srow = pl.ds(pl.multiple_of(hk * REC_DV, REC_DV), REC_DV)
            vrow = pl.ds(pl.multiple_of(h * REC_DV, REC_DV), REC_DV)
            s_new = (a_ref[idx, pl.ds(hk, 1), :] * s_in_ref[srow, :]
                     + k_ref[idx, pl.ds(hk, 1), :] * v_ref[idx, vrow, :])
            s_out_ref[srow, :] = s_new
            oacc_ref[vrow, :] += q_ref[idx, pl.ds(hk, 1), :] * s_new
            return carry

        lax.fori_loop(0, dk, body, 0)

    run(0, a_gla.T, gqk[:, dk:].T, (gqk[:, :dk] * (REC_DK ** -0.5)).T, col(C_GV).T,
        sg_in_ref, sg_out_ref, og_ref)
    run(1, rdec_ref[...], rk.T, rq.T, col(C_RV).T, sr_in_ref, sr_out_ref, or_ref)

    outs = []
    for acc_ref, g_ref, gate, center in ((og_ref, gn_ref, col(C_GR), False), (or_ref, rn_ref, col(C_RG), True)):
        heads = []
        for h in range(N_HEAD):
            o_h = acc_ref[h * REC_DV:(h + 1) * REC_DV, :]
            if center:
                o_h = o_h - jnp.mean(o_h, axis=0, keepdims=True)
            heads.append(o_h * lax.rsqrt(jnp.mean(o_h * o_h, axis=0, keepdims=True) + EPS))
        normed = jnp.concatenate(heads, axis=0) * g_ref[...]
        outs.append(normed.T * _silu(gate))
    o_ref[...] = jnp.concatenate(outs, axis=1).astype(BF16)


def _rec_sample(proj, cos, sin, lw, s_gla, s_ret):
    bd = proj.shape[0]
    dk, dv = N_HEAD * REC_DK, N_HEAD * REC_DV
    to_lanes = lambda s: s.reshape(bd, dk * REC_DV).T
    rdec = jnp.broadcast_to(jnp.repeat(_ret_tables()[4], REC_DK)[:, None], (dk, bd))
    gn_col = lw['gla_norm'].reshape(dv, 1)
    rn_col = lw['ret_norm'].reshape(dv, 1)
    st = jax.ShapeDtypeStruct((dk * REC_DV, bd), F32)
    o, sg, sr = pl.pallas_call(
        _rec_sample_kernel,
        out_shape=[jax.ShapeDtypeStruct((bd, 2 * BRANCH_W), BF16), st, st],
        scratch_shapes=[pltpu.VMEM((2, dk, bd), F32), pltpu.VMEM((2, dk, bd), F32),
                        pltpu.VMEM((2, dk, bd), F32), pltpu.VMEM((2, dv, bd), F32),
                        pltpu.VMEM((dv, bd), F32), pltpu.VMEM((dv, bd), F32)],
        compiler_params=pltpu.CompilerParams(vmem_limit_bytes=VMEM_LIMIT),
        name="rec_sample",
    )(proj, cos, sin, lw['w2'], lw['gla_b'], gn_col, rn_col, rdec, to_lanes(s_gla), to_lanes(s_ret))
    from_lanes = lambda s: s.T.reshape(bd, N_HEAD, REC_DK, REC_DV)
    return o, from_lanes(sg), from_lanes(sr)


def _prep_kernel(mcq_ref, mckv_ref, tail_ref, sq_ref, sk_ref, sv_ref, cos_ref, sin_ref,
                 nq_ref, nkv_ref, wuq_ref, wuk_ref,
                 qcat_ref, kcat_ref, ckv_ref, krope_ref, sqb_ref, skb_ref, svb_ref):
    cos, sin = cos_ref[...], sin_ref[...]
    c_q = _rms(mcq_ref[...], nq_ref[...]).astype(BF16)
    q_c = _dot(c_q, wuq_ref[...])
    q_lat = _dot(q_c[:, :N_HEAD * MLA_NOPE].astype(BF16), wuk_ref[...])
    q_rope = _rope_lanes(q_c[:, N_HEAD * MLA_NOPE:], cos, sin)
    c_kv = _rms(mckv_ref[...], nkv_ref[...])
    k_rope = _rope_lanes(tail_ref[...], cos, sin)
    lane = _iota(k_rope.shape, 1)
    ckv_ref[...] = c_kv
    krope_ref[...] = k_rope[:, :MLA_DR]
    kcat_ref[:, :MLA_DC] = c_kv.astype(BF16)
    one = jnp.where(lane == MLA_ONE - MLA_DC, 1.0, 0.0)
    kcat_ref[:, MLA_DC:] = jnp.where(lane < MLA_DR, k_rope, one).astype(BF16)
    for h in range(N_HEAD):
        qcat_ref[h, :, :MLA_DC] = q_lat[:, h * MLA_DC:(h + 1) * MLA_DC].astype(BF16)
        shifted = q_rope if h == 0 else pltpu.roll(q_rope, LANE - h * MLA_DR, 1)
        qcat_ref[h, :, MLA_DC:] = jnp.where(lane < MLA_DR, shifted, 0.0).astype(BF16)
    sqb_ref[...] = (sq_ref[...] * (SB_DH ** -0.5)).astype(BF16)
    skb_ref[...] = sk_ref[...].astype(BF16)
    svb_ref[...] = sv_ref[...].astype(BF16)


def _prep(proj, cos, sin, lw):
    m = proj.shape[0]
    tm = min(m, 512)
    blk256 = lambda j: pl.BlockSpec((tm, 256), lambda i, j=j: (i, j))
    blk128 = lambda j: pl.BlockSpec((tm, LANE), lambda i, j=j: (i, j))
    row = lambda w: pl.BlockSpec((tm, w), lambda i: (i, 0))
    full = lambda a: pl.BlockSpec(a.shape, lambda i: (0,) * a.ndim)
    consts = (lw['mla_norm_q'], lw['mla_norm_kv'], lw['w_uq'], lw['w_uk'])
    return pl.pallas_call(
        _prep_kernel,
        grid=(m // tm,),
        in_specs=[blk256(C_MCQ), blk128(C_MCKV), blk128(C_TAIL), blk256(C_SQ), blk256(C_SK), blk256(C_SV),
                  row(LANE), row(LANE)] + [full(a) for a in consts],
        out_specs=[pl.BlockSpec((N_HEAD, tm, 256), lambda i: (0, i, 0)), row(256), row(MLA_DC), row(MLA_DR),
                   row(256), row(256), row(256)],
        out_shape=[jax.ShapeDtypeStruct((N_HEAD, m, 256), BF16), jax.ShapeDtypeStruct((m, 256), BF16),
                   jax.ShapeDtypeStruct((m, MLA_DC), F32), jax.ShapeDtypeStruct((m, MLA_DR), F32),
                   jax.ShapeDtypeStruct((m, 256), BF16), jax.ShapeDtypeStruct((m, 256), BF16),
                   jax.ShapeDtypeStruct((m, 256), BF16)],
        compiler_params=_cparams(("arbitrary",)),
        name="mixer_prep",
    )(proj, proj, proj, proj, proj, proj, cos, sin, *consts)


_EXP2_SCALE = MLA_SCALE * math.log2(math.e)


def _mla_prompt_kernel(q_ref, k_ref, wuv_ref, o_ref, m_ref, acc_ref, *, tq, tk):
    i = pl.program_id(0)
    m_ref[...] = jnp.full_like(m_ref, -jnp.inf)
    acc_ref[...] = jnp.zeros_like(acc_ref)
    reps = tk // LANE

    def tile(j, masked):
        k = k_ref[pl.ds(pl.multiple_of(j * tk, tk), tk), :]
        if masked:
            keep = (j * tk + _iota((tq, tk), 1)) <= (i * tq + _iota((tq, tk), 0))
        for h in range(N_HEAD):
            rs = slice(h * tq, (h + 1) * tq)
            s = _dot_nt(q_ref[h], k)
            if masked:
                s = jnp.where(keep, s, -jnp.inf)
            m_old = m_ref[rs, :]
            m_new = jnp.maximum(m_old, jnp.max(s, axis=-1, keepdims=True))
            alpha = jnp.exp2((m_old - m_new) * _EXP2_SCALE)
            p = jnp.exp2((s - jnp.tile(m_new, (1, reps))) * _EXP2_SCALE)
            acc_ref[rs, :] = jnp.tile(alpha, (1, 2)) * acc_ref[rs, :] + _dot(p.astype(BF16), k)
            m_ref[rs, :] = m_new

    n_full = (i * tq) // tk

    def body(j, carry):
        tile(j, False)
        return carry

    lax.fori_loop(0, n_full, body, 0)
    tile(n_full, True)
    out = None
    for h in range(N_HEAD):
        acc = acc_ref[h * tq:(h + 1) * tq, :]
        o_lat = acc[:, :MLA_DC] / acc[:, MLA_ONE:MLA_ONE + 1]
        term = _dot(o_lat.astype(BF16), wuv_ref[h])
        out = term if out is None else out + term
    o_ref[...] = out.astype(BF16)


def _mla_prompt(qcat, kcat, wuv):
    l = kcat.shape[0]
    tq = min(l, MLA_TQ)
    tk = min(l, MLA_TK)
    rows = N_HEAD * tq
    return pl.pallas_call(
        functools.partial(_mla_prompt_kernel, tq=tq, tk=tk),
        grid=(l // tq,),
        in_specs=[pl.BlockSpec((N_HEAD, tq, 256), lambda i: (0, i, 0)),
                  pl.BlockSpec((l, 256), lambda i: (0, 0)),
                  pl.BlockSpec(wuv.shape, lambda i: (0, 0, 0))],
        out_specs=pl.BlockSpec((tq, BRANCH_W), lambda i: (i, 0)),
        out_shape=jax.ShapeDtypeStruct((l, BRANCH_W), BF16),
        scratch_shapes=[pltpu.VMEM((rows, LANE), F32), pltpu.VMEM((rows, 256), F32)],
        compiler_params=_cparams(("arbitrary",)),
        name="mla_prompt",
    )(qcat, kcat, wuv)


def _strict_suffix_matrix(n):
    return (_iota((n, n), 0) > _iota((n, n), 1)).astype(BF16)


def _sb_block(q_h, k_h, v_h, carry, mask, suffix):
    z = _dot_nt(q_h, k_h)
    lsig, l1m = _log_sigmoid_pair(z)
    if mask is not None:
        l1m = jnp.where(mask, l1m, 0.0)
    cum = _dot_exact_rhs(l1m, suffix)
    a = jnp.exp(lsig + cum + carry)
    if mask is not None:
        a = jnp.where(mask, a, 0.0)
    return _dot(a.astype(BF16), v_h), carry + jnp.sum(l1m, axis=-1, keepdims=True)


def _sb_prompt_kernel(q_ref, k_ref, v_ref, o_ref, *, tq):
    i = pl.program_id(0)
    suffix = _strict_suffix_matrix(tq)
    q = q_ref[...]
    qpos = i * tq + _iota((tq, tq), 0)

    def cond(state):
        return state[0]

    def body(state):
        _, j, carries, accs = state
        rows = pl.ds(pl.multiple_of(j * tq, tq), tq)
        mask = (j * tq + _iota((tq, tq), 1)) < qpos
        new_c, new_a, alive = [], [], None
        for h in range(N_HEAD):
            hs = slice(h * SB_DH, (h + 1) * SB_DH)
            pv, c_h = _sb_block(q[:, hs], k_ref[rows, hs], v_ref[rows, hs], carries[h], mask, suffix)
            new_c.append(c_h)
            new_a.append(accs[h] + pv)
            top = jnp.max(c_h)
            alive = top if alive is None else jnp.maximum(alive, top)
        go = jnp.logical_and(j > 0, alive > SB_DEAD)
        return go, j - 1, tuple(new_c), tuple(new_a)

    init = (i >= 0, i, tuple(jnp.zeros((tq, 1), F32) for _ in range(N_HEAD)),
            tuple(jnp.zeros((tq, SB_DH), F32) for _ in range(N_HEAD)))
    _, _, _, accs = lax.while_loop(cond, body, init)
    o_ref[...] = jnp.concatenate(accs, axis=1).astype(BF16)


def _sb_prompt(sq, sk, sv):
    l = sq.shape[0]
    tq = min(l, SB_TQ)
    full = pl.BlockSpec((l, 256), lambda i: (0, 0))
    return pl.pallas_call(
        functools.partial(_sb_prompt_kernel, tq=tq),
        grid=(l // tq,),
        in_specs=[pl.BlockSpec((tq, 256), lambda i: (i, 0)), full, full],
        out_specs=pl.BlockSpec((tq, BRANCH_W), lambda i: (i, 0)),
        out_shape=jax.ShapeDtypeStruct((l, BRANCH_W), BF16),
        compiler_params=_cparams(("arbitrary",)),
        name="sb_prompt",
    )(sq, sk, sv)


def _mla_decode_kernel(pt_ref, q_ref, kself_ref, wuv_ref, ckv_hbm, kr_hbm, o_ref,
                       ckv_buf, kr_buf, sem, *, layer, n_pages, chunk, page):
    s = pl.program_id(0)
    n_seq = pl.num_programs(0)
    n_chunk = n_pages // chunk
    total = n_seq * n_chunk

    def copies(g, slot):
        seq, c = g // n_chunk, g % n_chunk
        out = []
        for p in range(chunk):
            pid = pt_ref[seq, c * chunk + p]
            dst = pl.ds(p * page, page)
            out.append(pltpu.make_async_copy(ckv_hbm.at[layer, pid], ckv_buf.at[slot, dst], sem.at[0, slot]))
            out.append(pltpu.make_async_copy(kr_hbm.at[layer, pid], kr_buf.at[slot, :, dst], sem.at[1, slot]))
        return out

    @pl.when(s == 0)
    def _():
        for cp in copies(0, 0):
            cp.start()

    q = q_ref[0]
    q_lat, q_rope = q[:, :MLA_DC], q[:, MLA_DC:MLA_DC + MLA_DR]
    nq = q.shape[0]

    def step(c, state):
        m_old, l_old, acc = state
        g = s * n_chunk + c
        slot = g % 2

        @pl.when(g + 1 < total)
        def _():
            for cp in copies(g + 1, 1 - slot):
                cp.start()

        for cp in copies(g, slot):
            cp.wait()
        kc = ckv_buf[slot].astype(BF16)
        sc = (_dot_nt(q_lat, kc) + _dot(q_rope, kr_buf[slot].astype(BF16))) * MLA_SCALE
        m_new = jnp.maximum(m_old, jnp.max(sc, axis=-1, keepdims=True))
        alpha = jnp.exp(m_old - m_new)
        p = jnp.exp(sc - m_new)
        return (m_new, alpha * l_old + jnp.sum(p, axis=-1, keepdims=True),
                alpha * acc + _dot(p.astype(BF16), kc))

    init = (jnp.full((nq, 1), -jnp.inf, F32), jnp.zeros((nq, 1), F32), jnp.zeros((nq, MLA_DC), F32))
    m_old, l_old, acc = lax.fori_loop(0, n_chunk, step, init)
    kself = kself_ref[0].astype(F32)
    s_self = jnp.sum(q.astype(F32) * kself, axis=-1, keepdims=True) * MLA_SCALE
    m_new = jnp.maximum(m_old, s_self)
    alpha = jnp.exp(m_old - m_new)
    p_self = jnp.exp(s_self - m_new).astype(BF16).astype(F32)
    l_new = alpha * l_old + p_self
    o_lat = ((alpha * acc + p_self * kself[:, :MLA_DC]) / l_new).astype(BF16)
    out = None
    for h in range(N_HEAD):
        term = jnp.where(_iota((nq, BRANCH_W), 0) == h, _dot(o_lat, wuv_ref[h]), 0.0)
        out = term if out is None else out + term
    o_ref[0] = jnp.sum(out, axis=0, keepdims=True).astype(BF16)


def _mla_decode(page_table, q16, kself, wuv, cache_ckv, cache_kr_t, layer):
    bd, n_pages = page_table.shape
    page = cache_ckv.shape[2]
    chunk = math.gcd(n_pages, MLA_DECODE_CHUNK)
    grid_spec = pltpu.PrefetchScalarGridSpec(
        num_scalar_prefetch=1,
        grid=(bd,),
        in_specs=[pl.BlockSpec((1,) + q16.shape[1:], lambda s, pt: (s, 0, 0)),
                  pl.BlockSpec((1, 1, 256), lambda s, pt: (s, 0, 0)),
                  pl.BlockSpec(wuv.shape, lambda s, pt: (0, 0, 0)),
                  pl.BlockSpec(memory_space=pl.ANY),
                  pl.BlockSpec(memory_space=pl.ANY)],
        out_specs=pl.BlockSpec((1, 1, BRANCH_W), lambda s, pt: (s, 0, 0)),
        scratch_shapes=[pltpu.VMEM((2, chunk * page, MLA_DC), F32),
                        pltpu.VMEM((2, MLA_DR, chunk * page), F32),
                        pltpu.SemaphoreType.DMA((2, 2))],
    )
    out = pl.pallas_call(
        functools.partial(_mla_decode_kernel, layer=layer, n_pages=n_pages, chunk=chunk, page=page),
        grid_spec=grid_spec,
        out_shape=jax.ShapeDtypeStruct((bd, 1, BRANCH_W), BF16),
        compiler_params=_cparams(("arbitrary",)),
        name="mla_decode",
    )(page_table, q16, kself, wuv, cache_ckv, cache_kr_t)
    return out.reshape(bd, BRANCH_W)


def _sb_decode_kernel(pt_ref, q_ref, k_hbm, v_hbm, o_ref, kbuf, vbuf, kx, vx, sem, semx,
                      *, layer, n_pages, group, page):
    blk = pl.program_id(0)
    suffix = _strict_suffix_matrix(page)

    def head_copies(g, j):
        pid = pt_ref[blk * group + g, n_pages - 1 - j]
        return (pltpu.make_async_copy(k_hbm.at[layer, pid], kbuf.at[g, j], sem.at[0, g, j]),
                pltpu.make_async_copy(v_hbm.at[layer, pid], vbuf.at[g, j], sem.at[1, g, j]))

    for g in range(group):
        for j in range(2):
            for cp in head_copies(g, j):
                cp.start()

    def page_step(q8, kref, vref, carry, accs):
        z = None
        for h in range(N_HEAD):
            term = _dot(q8[h], kref[h].astype(BF16))
            z = term if z is None else z + term
        lsig, l1m = _log_sigmoid_pair(z)
        cum = _dot_exact_rhs(l1m, suffix)
        a = jnp.exp(lsig + cum + carry).astype(BF16)
        new_accs = tuple(accs[h] + _dot_nt(a, vref[h].astype(BF16)) for h in range(N_HEAD))
        return carry + jnp.sum(l1m, axis=-1, keepdims=True), new_accs

    rows8 = _iota((8, SB_DH), 0)
    for g in range(group):
        seq = blk * group + g
        qrow = q_ref[g].astype(F32)
        q8 = [jnp.where(rows8 == h, jnp.broadcast_to(qrow[:, h * SB_DH:(h + 1) * SB_DH], (8, SB_DH)),
                        0.0).astype(BF16) for h in range(N_HEAD)]
        carry = jnp.zeros((8, 1), F32)
        accs = tuple(jnp.zeros((8, SB_DH), F32) for _ in range(N_HEAD))
        for cp in head_copies(g, 0):
            cp.wait()
        carry, accs = page_step(q8, kbuf.at[g, 0], vbuf.at[g, 0], carry, accs)
        for cp in head_copies(g, 1):
            cp.wait()
        carry, accs = page_step(q8, kbuf.at[g, 1], vbuf.at[g, 1], carry, accs)

        def cond(state):
            return state[0]

        def body(state):
            _, j, c, a = state
            pid = pt_ref[seq, n_pages - 1 - j]
            ck = pltpu.make_async_copy(k_hbm.at[layer, pid], kx, semx.at[0])
            cv = pltpu.make_async_copy(v_hbm.at[layer, pid], vx, semx.at[1])
            ck.start()
            cv.start()
            ck.wait()
            cv.wait()
            c, a = page_step(q8, kx, vx, c, a)
            go = jnp.logical_and(j + 1 < n_pages, jnp.max(c[:N_HEAD]) > SB_DEAD)
            return go, j + 1, c, a

        go0 = jnp.logical_and(n_pages > 2, jnp.max(carry[:N_HEAD]) > SB_DEAD)
        _, _, carry, accs = lax.while_loop(cond, body, (go0, jnp.int32(2), carry, accs))
        out = jnp.concatenate(accs, axis=1)
        keep = _iota(out.shape, 0) == _iota(out.shape, 1) // SB_DH
        o_ref[g] = jnp.sum(jnp.where(keep, out, 0.0), axis=0, keepdims=True).astype(BF16)


def _sb_decode(page_table, sq, cache_k_t, cache_v_t, layer):
    bd, n_pages = page_table.shape
    assert n_pages >= 2
    page = cache_k_t.shape[4]
    group = math.gcd(bd, 8)
    slab = (N_HEAD, SB_DH, page)
    grid_spec = pltpu.PrefetchScalarGridSpec(
        num_scalar_prefetch=1,
        grid=(bd // group,),
        in_specs=[pl.BlockSpec((group, 1, 256), lambda b, pt: (b, 0, 0)),
                  pl.BlockSpec(memory_space=pl.ANY),
                  pl.BlockSpec(memory_space=pl.ANY)],
        out_specs=pl.BlockSpec((group, 1, BRANCH_W), lambda b, pt: (b, 0, 0)),
        scratch_shapes=[pltpu.VMEM((group, 2) + slab, F32), pltpu.VMEM((group, 2) + slab, F32),
                        pltpu.VMEM(slab, F32), pltpu.VMEM(slab, F32),
                        pltpu.SemaphoreType.DMA((2, group, 2)), pltpu.SemaphoreType.DMA((2,))],
    )
    out = pl.pallas_call(
        functools.partial(_sb_decode_kernel, layer=layer, n_pages=n_pages, group=group, page=page),
        grid_spec=grid_spec,
        out_shape=jax.ShapeDtypeStruct((bd, 1, BRANCH_W), BF16),
        compiler_params=_cparams(("arbitrary",)),
        name="sb_decode",
    )(page_table, sq.reshape(bd, 1, 256), cache_k_t, cache_v_t)
    return out.reshape(bd, BRANCH_W)


def _merge_kernel(x_ref, oab_ref, oc_ref, od_ref, gpre_ref, gpost_ref, wg_ref, wb_ref, wo_ref, y_ref):
    x = x_ref[...]
    h = _rms(x, gpre_ref[...]).astype(BF16)
    oab = oab_ref[...]
    branches = (oab[:, :BRANCH_W], oab[:, BRANCH_W:], oc_ref[...], od_ref[...])
    d = x.shape[1]
    merged = None
    for n in range(N_BRANCH):
        gate = jax.nn.sigmoid(_dot(h, wg_ref[:, n * d:(n + 1) * d]))
        term = gate * _dot(branches[n], wb_ref[n])
        merged = term if merged is None else merged + term
    mix = _dot(merged.astype(BF16), wo_ref[...])
    y_ref[...] = x + _rms(mix, gpost_ref[...])


def _merge(x, oab, oc, od, lw):
    m, d = x.shape
    tm = min(m, 512)
    row = lambda w: pl.BlockSpec((tm, w), lambda i: (i, 0))
    full = lambda a: pl.BlockSpec(a.shape, lambda i: (0,) * a.ndim)
    consts = (lw['norm_mix_pre'], lw['norm_mix_post'], lw['w_gate'], lw['w_branch'], lw['w_out'])
    return pl.pallas_call(
        _merge_kernel,
        grid=(m // tm,),
        in_specs=[row(d), row(2 * BRANCH_W), row(BRANCH_W), row(BRANCH_W)] + [full(a) for a in consts],
        out_specs=row(d),
        out_shape=jax.ShapeDtypeStruct((m, d), F32),
        compiler_params=_cparams(("arbitrary",)),
        name="merge",
    )(x, oab, oc, od, *consts)


def _gelu(x):
    return 0.5 * x * (1.0 + lax.erf(x * (2.0 ** -0.5)))


def _ffn_tail(c, x_ref, gpost_ref, acc_ref, y_ref):
    @pl.when(c == pl.num_programs(1) - 1)
    def _():
        y_ref[...] = x_ref[...] + _rms(acc_ref[...], gpost_ref[...])


def _ffn_prompt_kernel(x_ref, gpre_ref, gpost_ref, wg_ref, wu_ref, cw_ref, cb_ref, wo_ref,
                       y_ref, tail_ref, h_ref, acc_ref, prev_ref):
    i, c = pl.program_id(0), pl.program_id(1)

    @pl.when(c == 0)
    def _():
        h_ref[...] = _rms(x_ref[...], gpre_ref[...]).astype(BF16)
        acc_ref[...] = jnp.zeros_like(acc_ref)

    @pl.when(i == 0)
    def _():
        prev_ref[c] = jnp.zeros(prev_ref.shape[1:], F32)

    h = h_ref[...]
    gp = _dot(h, wg_ref[...])
    tm = gp.shape[0]
    prev = prev_ref[c]
    row = _iota(gp.shape, 0)
    back1 = jnp.where(row == 0, prev[7:8], pltpu.roll(gp, 1, 0))
    back2 = jnp.where(row == 0, prev[6:7], jnp.where(row == 1, prev[7:8], pltpu.roll(gp, 2, 0)))
    cw = cw_ref[...]
    conv = cb_ref[...] + cw[0:1] * back2 + cw[1:2] * back1 + cw[2:3] * gp
    act = _gelu(conv) * _dot(h, wu_ref[...])
    acc_ref[...] += _dot(act.astype(BF16), wo_ref[...])
    prev_ref[c] = gp[tm - 8:tm]
    tail_ref[0] = gp[tm - (CONV_W - 1):tm]
    _ffn_tail(c, x_ref, gpost_ref, acc_ref, y_ref)


def _ffn_sample_kernel(x_ref, gpre_ref, gpost_ref, wg_ref, wu_ref, cw_ref, cb_ref, wo_ref, s0_ref, s1_ref,
                       y_ref, gp_ref, h_ref, acc_ref):
    c = pl.program_id(1)

    @pl.when(c == 0)
    def _():
        h_ref[...] = _rms(x_ref[...], gpre_ref[...]).astype(BF16)
        acc_ref[...] = jnp.zeros_like(acc_ref)

    h = h_ref[...]
    gp = _dot(h, wg_ref[...])
    cw = cw_ref[...]
    conv = cb_ref[...] + cw[0:1] * s0_ref[...] + cw[1:2] * s1_ref[...] + cw[2:3] * gp
    act = _gelu(conv) * _dot(h, wu_ref[...])
    acc_ref[...] += _dot(act.astype(BF16), wo_ref[...])
    gp_ref[...] = gp
    _ffn_tail(c, x_ref, gpost_ref, acc_ref, y_ref)


def _ffn(x, lw, state=None):
    m, d = x.shape
    dff = lw['w_ffn_out'].shape[0]
    nck = 2
    ck = dff // nck
    tm = min(m, 512)
    xrow = pl.BlockSpec((tm, d), lambda i, c: (i, 0))
    vec = lambda a: pl.BlockSpec(a.shape, lambda i, c: (0,) * a.ndim)
    in_specs = [xrow, vec(lw['norm_ffn_pre']), vec(lw['norm_ffn_post']),
                pl.BlockSpec((d, ck), lambda i, c: (0, c)),
                pl.BlockSpec((d, ck), lambda i, c: (0, c + nck)),
                pl.BlockSpec((CONV_W, ck), lambda i, c: (0, c)),
                pl.BlockSpec((1, ck), lambda i, c: (0, c)),
                pl.BlockSpec((ck, d), lambda i, c: (c, 0))]
    args = [x, lw['norm_ffn_pre'], lw['norm_ffn_post'], lw['w_ffn_in'], lw['w_ffn_in'],
            lw['ffn_conv_w'], lw['ffn_conv_b'], lw['w_ffn_out']]
    scratch = [pltpu.VMEM((tm, d), BF16), pltpu.VMEM((tm, d), F32)]
    if state is None:
        kern = _ffn_prompt_kernel
        out_specs = [xrow, pl.BlockSpec((1, CONV_W - 1, ck), lambda i, c: (i, 0, c))]
        out_shape = [jax.ShapeDtypeStruct((m, d), F32), jax.ShapeDtypeStruct((m // tm, CONV_W - 1, dff), F32)]
        scratch = scratch + [pltpu.VMEM((nck, 8, ck), F32)]
    else:
        kern = _ffn_sample_kernel
        srow = pl.BlockSpec((tm, ck), lambda i, c: (i, c))
        in_specs += [srow, srow]
        args += [state[:, 0, :], state[:, 1, :]]
        out_specs = [xrow, srow]
        out_shape = [jax.ShapeDtypeStruct((m, d), F32), jax.ShapeDtypeStruct((m, dff), F32)]
    return pl.pallas_call(
        kern,
        grid=(m // tm, nck),
        in_specs=in_specs,
        out_specs=out_specs,
        out_shape=out_shape,
        scratch_shapes=scratch,
        compiler_params=_cparams(("arbitrary", "arbitrary")),
        name="conv_ffn",
    )(*args)


def _rope_tables(pos):
    half = MLA_DR // 2
    inv_freq = jnp.exp(-math.log(ROPE_BASE) * jnp.arange(half, dtype=F32) / half)
    ang = pos.astype(F32)[:, None] * inv_freq[None, :]
    cos, sin = jnp.cos(ang), jnp.sin(ang)
    reps = LANE // MLA_DR
    return (jnp.tile(jnp.concatenate([cos, cos], axis=1), (1, reps)),
            jnp.tile(jnp.concatenate([-sin, sin], axis=1), (1, reps)))


def _layer_weights(l, w_in, gla_w_gate2, gla_b_gate, gla_norm, ret_norm, mla_norm_q, mla_norm_kv, mla_w_uq,
                   mla_w_uk, mla_w_uv, w_branch, w_out, w_ffn_in, ffn_conv_w, ffn_conv_b, w_ffn_out,
                   norm_mix_pre, norm_mix_post, norm_ffn_pre, norm_ffn_post):
    d = w_in.shape[1]
    hk, hv = N_HEAD * REC_DK, N_HEAD * REC_DV
    sizes = (hk, hk, hv, GLA_RANK, hv, hk, hk, hv, hv, MLA_DQ, MLA_DC, MLA_DR,
             N_HEAD * SB_DH, N_HEAD * SB_DH, N_HEAD * SB_DH, N_BRANCH * d)
    pts, acc = [], 0
    for s in sizes[:-1]:
        acc += s
        pts.append(acc)
    (g_q, g_k, g_v, g_lr, g_r, r_q, r_k, r_v, r_g, m_cq, m_ckv, m_kr, s_q, s_k, s_v, w_gate) = jnp.split(
        w_in[l], pts, axis=1)
    pad = jnp.zeros((d, LANE - MLA_DR - GLA_RANK), F32)
    w_mix = jnp.concatenate([g_q, g_k, g_v, g_r, r_q, r_k, r_v, r_g, m_cq, s_q, s_k, s_v, m_ckv,
                             m_kr, g_lr, pad], axis=1)
    assert w_mix.shape[1] == D_MIX
    w2 = jnp.zeros((LANE, hk), F32).at[TAIL_GLR:TAIL_GLR + GLA_RANK].set(gla_w_gate2[l])
    uq = mla_w_uq[l].reshape(MLA_DQ, N_HEAD, MLA_NOPE + MLA_DR)
    w_uq = jnp.concatenate([uq[:, :, :MLA_NOPE].reshape(MLA_DQ, -1), uq[:, :, MLA_NOPE:].reshape(MLA_DQ, -1)], axis=1)
    uk = jnp.transpose(mla_w_uk[l], (1, 2, 0))
    w_uk = jnp.zeros((N_HEAD, MLA_NOPE, N_HEAD, MLA_DC), F32)
    for h in range(N_HEAD):
        w_uk = w_uk.at[h, :, h, :].set(uk[h])
    w_uk = w_uk.reshape(N_HEAD * MLA_NOPE, N_HEAD * MLA_DC)
    uv = jnp.transpose(mla_w_uv[l], (1, 0, 2))
    w_uv = jnp.zeros((N_HEAD, MLA_DC, N_HEAD, MLA_DV), F32)
    for h in range(N_HEAD):
        w_uv = w_uv.at[h, :, h, :].set(uv[h])
    w_uv = w_uv.reshape(N_HEAD, MLA_DC, N_HEAD * MLA_DV)
    row = lambda a: a[l].reshape(1, -1)
    return {
        'w_mix': w_mix.astype(BF16), 'w_gate': w_gate.astype(BF16), 'w2': w2.astype(BF16),
        'gla_b': row(gla_b_gate), 'gla_norm': row(gla_norm), 'ret_norm': row(ret_norm),
        'mla_norm_q': row(mla_norm_q), 'mla_norm_kv': row(mla_norm_kv),
        'w_uq': w_uq.astype(BF16), 'w_uk': w_uk.astype(BF16), 'w_uv': w_uv.astype(BF16),
        'w_branch': w_branch[l].astype(BF16), 'w_out': w_out[l].astype(BF16),
        'w_ffn_in': w_ffn_in[l].astype(BF16), 'ffn_conv_w': ffn_conv_w[l], 'ffn_conv_b': row(ffn_conv_b),
        'w_ffn_out': w_ffn_out[l].astype(BF16),
        'norm_mix_pre': row(norm_mix_pre), 'norm_mix_post': row(norm_mix_post),
        'norm_ffn_pre': row(norm_ffn_pre), 'norm_ffn_post': row(norm_ffn_post),
    }


def kernel(x_prompt, x_sample, cache_mla_ckv, cache_mla_krope, cache_sb_k, cache_sb_v, state_gla, state_ret, state_ffn_conv, page_table, w_in, gla_w_gate2, gla_b_gate, gla_norm, ret_norm, mla_norm_q, mla_norm_kv, mla_w_uq, mla_w_uk, mla_w_uv, w_branch, w_out, w_ffn_in, ffn_conv_w, ffn_conv_b, w_ffn_out, norm_mix_pre, norm_mix_post, norm_ffn_pre, norm_ffn_post):
    bp, seq, d = x_prompt.shape
    bd, dseq, _ = x_sample.shape
    assert bp == 1 and dseq == 1
    depth = w_in.shape[0]
    past = page_table.shape[1] * cache_mla_ckv.shape[2]
    cos_p, sin_p = _rope_tables(jnp.arange(seq, dtype=jnp.int32))
    cos_s, sin_s = _rope_tables(jnp.full((bd,), past, jnp.int32))

    cache_kr_t = jnp.transpose(cache_mla_krope, (0, 1, 3, 2))
    cache_k_t = jnp.transpose(cache_sb_k, (0, 1, 3, 4, 2))
    cache_v_t = jnp.transpose(cache_sb_v, (0, 1, 3, 4, 2))

    xp = x_prompt.reshape(seq, d)
    xs = x_sample.reshape(bd, d)
    p_rows = [[] for _ in range(7)]
    s_rows = [[] for _ in range(7)]
    for l in range(depth):
        lw = _layer_weights(l, w_in, gla_w_gate2, gla_b_gate, gla_norm, ret_norm, mla_norm_q, mla_norm_kv,
                            mla_w_uq, mla_w_uk, mla_w_uv, w_branch, w_out, w_ffn_in, ffn_conv_w, ffn_conv_b,
                            w_ffn_out, norm_mix_pre, norm_mix_post, norm_ffn_pre, norm_ffn_post)
        proj = _norm_matmul(xp, lw['norm_mix_pre'], lw['w_mix'])
        oab, gla_new, ret_new = _rec_prompt(proj, cos_p, sin_p, lw)
        qcat, kcat, ckv, krope, sqb, skb, svb = _prep(proj, cos_p, sin_p, lw)
        oc = _mla_prompt(qcat, kcat, lw['w_uv'])
        od = _sb_prompt(sqb, skb, svb)
        xp = _merge(xp, oab, oc, od, lw)
        xp, conv_tails = _ffn(xp, lw)
        conv_tail = conv_tails[-1]
        sk_cols = slice(C_SK * 256, (C_SK + 1) * 256)
        sv_cols = slice(C_SV * 256, (C_SV + 1) * 256)
        for i, a in enumerate((ckv.reshape(1, seq, MLA_DC), krope.reshape(1, seq, MLA_DR),
                               proj[:, sk_cols].reshape(1, seq, N_HEAD, SB_DH),
                               proj[:, sv_cols].reshape(1, seq, N_HEAD, SB_DH),
                               gla_new[None], ret_new[None], conv_tail[None])):
            p_rows[i].append(a)

        proj = _norm_matmul(xs, lw['norm_mix_pre'], lw['w_mix'])
        oab, gla_new, ret_new = _rec_sample(proj, cos_s, sin_s, lw, state_gla[l], state_ret[l])
        qcat, kcat, ckv, krope, sqb, skb, svb = _prep(proj, cos_s, sin_s, lw)
        q16 = jnp.zeros((bd, 16, 256), BF16).at[:, :N_HEAD].set(jnp.transpose(qcat, (1, 0, 2)))
        oc = _mla_decode(page_table, q16, kcat.reshape(bd, 1, 256), lw['w_uv'], cache_mla_ckv, cache_kr_t, l)
        od = _sb_decode(page_table, sqb, cache_k_t, cache_v_t, l)
        xs = _merge(xs, oab, oc, od, lw)
        xs, gate_pre = _ffn(xs, lw, state_ffn_conv[l])
        conv_new = jnp.stack([state_ffn_conv[l][:, 1, :], gate_pre], axis=1)
        for i, a in enumerate((ckv.reshape(bd, 1, MLA_DC), krope.reshape(bd, 1, MLA_DR),
                               proj[:, sk_cols].reshape(bd, 1, N_HEAD, SB_DH),
                               proj[:, sv_cols].reshape(bd, 1, N_HEAD, SB_DH),
                               gla_new, ret_new, conv_new)):
            s_rows[i].append(a)

    outs_p = [jnp.stack(a, axis=0) for a in p_rows]
    outs_s = [jnp.stack(a, axis=0) for a in s_rows]
    return (xp.reshape(1, seq, d), xs.reshape(bd, 1, d), *outs_p, *outs_s)
```

```python
import functools
import math

import jax
import jax.numpy as jnp
from jax import lax
from jax.experimental import pallas as pl
from jax.experimental.pallas import tpu as pltpu

F32 = jnp.float32
BF16 = jnp.bfloat16

N_HEAD = 4
REC_DK = 32
REC_DV = 64
GLA_RANK = 16
GLA_TAU = 16.0
MLA_DQ = 256
MLA_DC = 128
MLA_DR = 32
MLA_NOPE = 64
MLA_DV = 64
MLA_SCALE = (MLA_NOPE + MLA_DR) ** -0.5
SB_DH = 64
N_BRANCH = 4
BRANCH_W = 256
CONV_W = 3
ROPE_BASE = 10000.0
EPS = 1e-6
REC_CHUNK = 64
SUB = 16
SB_DEAD = -104.0
MLA_DECODE_CHUNK = 32
MLA_DECODE_SLOTS = 3
MLA_ONE = MLA_DC + MLA_DR
SB_TQ = 256
MLA_TQ, MLA_TK = 512, 1024

LANE = 128
VMEM_LIMIT = 52 * 1024 * 1024

C_GQK, C_GV, C_GR, C_RQK, C_RV, C_RG, C_MCQ, C_SQ, C_SK, C_SV = range(10)
C_MCKV = 20
C_TAIL = 21
D_MIX = 22 * LANE
TAIL_GLR = MLA_DR


def _cparams(sem):
    return pltpu.CompilerParams(dimension_semantics=sem, vmem_limit_bytes=VMEM_LIMIT)


def _dot(a, b):
    return jnp.dot(a, b, preferred_element_type=F32)


def _dot_nt(a, b):
    return lax.dot_general(a, b, (((1,), (1,)), ((), ())), preferred_element_type=F32)


def _dot_tn(a, b):
    return lax.dot_general(a, b, (((0,), (0,)), ((), ())), preferred_element_type=F32)


def _split3(x):
    hi = x.astype(BF16)
    r1 = x - hi.astype(F32)
    mid = r1.astype(BF16)
    lo = (r1 - mid.astype(F32)).astype(BF16)
    return hi, mid, lo


def _dot_exact_rhs(x, m, terms=3):
    hi, mid, lo = _split3(x)
    out = _dot(hi, m) + _dot(mid, m)
    return out + _dot(lo, m) if terms == 3 else out


def _dot_exact_lhs(m, x):
    hi, mid, lo = _split3(x)
    return _dot(m, hi) + _dot(m, mid) + _dot(m, lo)


def _log_sigmoid_pair(z):
    t = jnp.log1p(jnp.exp(-jnp.abs(z)))
    return jnp.minimum(z, 0.0) - t, -jnp.maximum(z, 0.0) - t


def _rms(x, g):
    return x * lax.rsqrt(jnp.mean(x * x, axis=-1, keepdims=True) + EPS) * g


def _iota(shape, dim):
    return lax.broadcasted_iota(jnp.int32, shape, dim)


def _rope_lanes(x, cos, sin_signed):
    lane = _iota(x.shape, 1)
    swapped = jnp.where((lane % 32) < 16, pltpu.roll(x, 112, 1), pltpu.roll(x, 16, 1))
    return x * cos + swapped * sin_signed


def _head_mask(shape, lane_w, h):
    lane = _iota(shape, len(shape) - 1)
    return (lane >= h * lane_w) & (lane < (h + 1) * lane_w)


def _norm_matmul_kernel(x_ref, g_ref, w_ref, o_ref, h_ref):
    @pl.when(pl.program_id(1) == 0)
    def _():
        h_ref[...] = _rms(x_ref[...], g_ref[...]).astype(BF16)

    o_ref[...] = _dot(h_ref[...], w_ref[...])


def _norm_matmul(x, g, w):
    m, d = x.shape
    n = w.shape[1]
    tm = min(m, 1024)
    tn = n // 2
    return pl.pallas_call(
        _norm_matmul_kernel,
        grid=(m // tm, n // tn),
        in_specs=[pl.BlockSpec((tm, d), lambda i, j: (i, 0)),
                  pl.BlockSpec((1, d), lambda i, j: (0, 0)),
                  pl.BlockSpec((d, tn), lambda i, j: (0, j))],
        out_specs=pl.BlockSpec((tm, tn), lambda i, j: (i, j)),
        out_shape=jax.ShapeDtypeStruct((m, n), F32),
        scratch_shapes=[pltpu.VMEM((tm, d), BF16)],
        compiler_params=_cparams(("arbitrary", "arbitrary")),
        name="norm_matmul",
    )(x, g, w)


def _block_diag_mask():
    r = _iota((N_HEAD * REC_DK, N_HEAD * REC_DV), 0) // REC_DK
    c = _iota((N_HEAD * REC_DK, N_HEAD * REC_DV), 1) // REC_DV
    return r == c


def _head_avg_matrix():
    r = _iota((N_HEAD * REC_DV, N_HEAD * REC_DV), 0) // REC_DV
    c = _iota((N_HEAD * REC_DV, N_HEAD * REC_DV), 1) // REC_DV
    return jnp.where(r == c, 1.0 / REC_DV, 0.0).astype(BF16)


def _head_rms_lanes(o, g, avg):
    ms = _dot_exact_rhs(o * o, avg)
    return o * lax.rsqrt(ms + EPS) * g


def _head_groupnorm_lanes(o, g, avg):
    c = o - _dot_exact_rhs(o, avg)
    var = _dot_exact_rhs(c * c, avg)
    return c * lax.rsqrt(var + EPS) * g


def _silu(x):
    return x * jax.nn.sigmoid(x)


def _stack_heads(x, lane_w):
    return jnp.concatenate(
        [jnp.where(_head_mask(x.shape, lane_w, h), x, 0.0) for h in range(N_HEAD)], axis=0)


def _unstack_heads(o, rows, lane_w):
    acc = None
    for h in range(N_HEAD):
        blk = o[h * rows:(h + 1) * rows, :]
        term = jnp.where(_head_mask(blk.shape, lane_w, h), blk, 0.0)
        acc = term if acc is None else acc + term
    return acc


def _gla_chunk(q, k, v, log_a, s_gla, tri, expand, bmask):
    c = REC_CHUNK
    b = _dot_exact_lhs(tri, log_a)
    vb = v.astype(BF16)
    o = _dot((q * jnp.exp(b)).astype(BF16), s_gla.astype(BF16))
    row = _iota((SUB, N_HEAD * REC_DK), 0)
    parts = []
    for blk in range(c // SUB):
        lo = blk * SUB
        q_i, b_i = q[lo:lo + SUB], b[lo:lo + SUB]
        prods = []
        for j in range(SUB):
            rel = jnp.where(row >= j, b_i - b[lo + j:lo + j + 1], -jnp.inf)
            prods.append(q_i * k[lo + j:lo + j + 1] * jnp.exp(rel))
        p_all = jnp.concatenate(prods, axis=0).astype(BF16)
        r_all = _dot(p_all, expand)
        o_i = None
        for j in range(SUB):
            term = r_all[j * SUB:(j + 1) * SUB] * v[lo + j:lo + j + 1]
            o_i = term if o_i is None else o_i + term
        if blk > 0:
            ref = b[lo - 1:lo]
            q_t = q_i * jnp.exp(b_i - ref)
            k_rows = _iota((c, N_HEAD * REC_DK), 0)
            k_t = jnp.where(k_rows < lo, k * jnp.exp(jnp.minimum(ref - b, 0.0)), 0.0)
            s = _dot_nt(_stack_heads(q_t, REC_DK).astype(BF16), k_t.astype(BF16))
            o_i = o_i + _unstack_heads(_dot(s.astype(BF16), vb), SUB, REC_DV)
        parts.append(o_i)
    o = o + jnp.concatenate(parts, axis=0)
    b_end = b[c - 1:c]
    decay_rows = jnp.broadcast_to(jnp.exp(b_end), (LANE, LANE)).T
    decay_rows = jnp.concatenate([decay_rows, decay_rows], axis=1)
    upd = _dot_tn((k * jnp.exp(b_end - b)).astype(BF16), vb)
    s_new = decay_rows * s_gla + jnp.where(bmask, upd, 0.0)
    return o, s_new


def _ret_chunk(q, k, v, s_ret, dmat, qdec, kdec, cdec, bmask):
    c = REC_CHUNK
    vb = v.astype(BF16)
    qb = q.astype(BF16)
    o_inter = _dot(qb, s_ret.astype(BF16)) * qdec
    s = _dot_nt(_stack_heads(q, REC_DK).astype(BF16), k.astype(BF16)) * dmat
    o_intra = _unstack_heads(_dot(s.astype(BF16), vb), c, REC_DV)
    upd = _dot_tn((k * kdec).astype(BF16), vb)
    s_new = cdec * s_ret + jnp.where(bmask, upd, 0.0)
    return o_inter + o_intra, s_new


def _rec_prompt_kernel(gqk_ref, gv_ref, gr_ref, rqk_ref, rv_ref, rg_ref, tail_ref, cos_ref, sin_ref,
                       w2_ref, bg_ref, gn_ref, rn_ref, dmat_ref, qdec_ref, kdec_ref, cdec_ref,
                       o_ref, sg_out_ref, sr_out_ref, sg_ref, sr_ref, *, tc):
    step = pl.program_id(0)

    @pl.when(step == 0)
    def _():
        sg_ref[...] = jnp.zeros_like(sg_ref)
        sr_ref[...] = jnp.zeros_like(sr_ref)

    c = REC_CHUNK
    tri = (_iota((c, c), 0) >= _iota((c, c), 1)).astype(BF16)
    bmask = _block_diag_mask()
    expand = bmask.astype(BF16)
    avg = _head_avg_matrix()
    dk = N_HEAD * REC_DK

    def chunk(ci, carry):
        rows = pl.ds(pl.multiple_of(ci * c, c), c)
        gqk = gqk_ref[rows, :]
        x = _dot(tail_ref[rows, :].astype(BF16), w2_ref[...]) + bg_ref[...]
        log_a = _log_sigmoid_pair(x)[0] * (1.0 / GLA_TAU)
        o_g, sg_new = _gla_chunk(gqk[:, :dk] * (REC_DK ** -0.5), gqk[:, dk:], gv_ref[rows, :], log_a,
                                 sg_ref[...], tri, expand, bmask)
        sg_ref[...] = sg_new
        o_a = _head_rms_lanes(o_g, gn_ref[...], avg) * _silu(gr_ref[rows, :])

        rqk = rqk_ref[rows, :]
        cos, sin = cos_ref[rows, :], sin_ref[rows, :]
        rq = _rope_lanes(rqk[:, :dk], cos, sin)
        rk = _rope_lanes(rqk[:, dk:], cos, sin) * (REC_DK ** -0.5)
        o_r, sr_new = _ret_chunk(rq, rk, rv_ref[rows, :], sr_ref[...], dmat_ref[...], qdec_ref[...],
                                 kdec_ref[...], cdec_ref[...], bmask)
        sr_ref[...] = sr_new
        o_b = _head_groupnorm_lanes(o_r, rn_ref[...], avg) * _silu(rg_ref[rows, :])
        o_ref[rows, :] = jnp.concatenate([o_a, o_b], axis=1).astype(BF16)
        return carry

    lax.fori_loop(0, tc // c, chunk, 0, unroll=2)

    @pl.when(step == pl.num_programs(0) - 1)
    def _():
        for h in range(N_HEAD):
            sg_out_ref[h] = sg_ref[h * REC_DK:(h + 1) * REC_DK, h * REC_DV:(h + 1) * REC_DV]
            sr_out_ref[h] = sr_ref[h * REC_DK:(h + 1) * REC_DK, h * REC_DV:(h + 1) * REC_DV]


def _ret_tables():
    c = REC_CHUNK
    log_g = jnp.log1p(-jnp.exp2(-5.0 - jnp.arange(N_HEAD, dtype=F32)))
    idx = jnp.arange(c, dtype=F32)
    rel = idx[:, None] - idx[None, :]
    decay = jnp.exp(jnp.where(rel[None] >= 0, rel[None] * log_g[:, None, None], -jnp.inf))
    q_dec = jnp.exp((idx[:, None] + 1.0) * log_g[None, :])
    k_dec = jnp.exp((c - 1.0 - idx[:, None]) * log_g[None, :])
    chunk_dec = jnp.exp(c * log_g)
    dmat = decay.reshape(N_HEAD * c, c)
    qdec = jnp.repeat(q_dec, REC_DV, axis=1)
    kdec = jnp.repeat(k_dec, REC_DK, axis=1)
    cdec = jnp.broadcast_to(jnp.repeat(chunk_dec, REC_DK)[:, None], (N_HEAD * REC_DK, N_HEAD * REC_DV))
    return dmat, qdec, kdec, cdec, jnp.exp(log_g)


def _rec_prompt(proj, cos, sin, lw):
    l = proj.shape[0]
    tc = min(l, 256)
    dmat, qdec, kdec, cdec, _ = _ret_tables()
    blk256 = lambda j: pl.BlockSpec((tc, 256), lambda i, j=j: (i, j))
    blk128 = lambda j: pl.BlockSpec((tc, LANE), lambda i, j=j: (i, j))
    row128 = pl.BlockSpec((tc, LANE), lambda i: (i, 0))
    full = lambda a: pl.BlockSpec(a.shape, lambda i: (0,) * a.ndim)
    consts = (lw['w2'], lw['gla_b'], lw['gla_norm'], lw['ret_norm'], dmat, qdec, kdec, cdec)
    state = jax.ShapeDtypeStruct((N_HEAD, REC_DK, REC_DV), F32)
    return pl.pallas_call(
        functools.partial(_rec_prompt_kernel, tc=tc),
        grid=(l // tc,),
        in_specs=[blk256(C_GQK), blk256(C_GV), blk256(C_GR), blk256(C_RQK), blk256(C_RV), blk256(C_RG),
                  blk128(C_TAIL), row128, row128] + [full(a) for a in consts],
        out_specs=[pl.BlockSpec((tc, 2 * BRANCH_W), lambda i: (i, 0)),
                   pl.BlockSpec(state.shape, lambda i: (0, 0, 0)),
                   pl.BlockSpec(state.shape, lambda i: (0, 0, 0))],
        out_shape=[jax.ShapeDtypeStruct((l, 2 * BRANCH_W), BF16), state, state],
        scratch_shapes=[pltpu.VMEM((N_HEAD * REC_DK, N_HEAD * REC_DV), F32),
                        pltpu.VMEM((N_HEAD * REC_DK, N_HEAD * REC_DV), F32)],
        compiler_params=_cparams(("arbitrary",)),
        name="rec_prompt",
    )(proj, proj, proj, proj, proj, proj, proj, cos, sin, *consts)


def _rec_sample_kernel(proj_ref, cos_ref, sin_ref, w2_ref, bg_ref, gn_ref, rn_ref, rdec_ref,
                       sg_in_ref, sr_in_ref, o_ref, sg_out_ref, sr_out_ref,
                       a_ref, k_ref, q_ref, v_ref, og_ref, or_ref):
    dk = N_HEAD * REC_DK
    dv = N_HEAD * REC_DV
    col = lambda j, w=256: proj_ref[:, j * w:(j + 1) * w]
    gqk, rqk = col(C_GQK), col(C_RQK)
    x = _dot(col(C_TAIL, LANE).astype(BF16), w2_ref[...]) + bg_ref[...]
    a_gla = jnp.exp(_log_sigmoid_pair(x)[0] * (1.0 / GLA_TAU))
    cos, sin = cos_ref[...], sin_ref[...]
    rq = _rope_lanes(rqk[:, :dk], cos, sin)
    rk = _rope_lanes(rqk[:, dk:], cos, sin) * (REC_DK ** -0.5)

    def run(idx, a_t, k_t, q_t, v_t, s_in_ref, s_out_ref, oacc_ref):
        a_ref[idx], k_ref[idx], q_ref[idx], v_ref[idx] = a_t, k_t, q_t, v_t
        oacc_ref[...] = jnp.zeros_like(oacc_ref)

        def body(hk, carry):
            h = hk // REC_DK
            srow = pl.ds(pl.multiple_of(hk * REC_DV, REC_DV), REC_DV)
            vrow = pl.ds(pl.multiple_of(h * REC_DV, REC_DV), REC_DV)
            s_new = (a_ref[idx, pl.ds(hk, 1), :] * s_in_ref[srow, :]
                     + k_ref[idx, pl.ds(hk, 1), :] * v_ref[idx, vrow, :])
            s_out_ref[srow, :] = s_new
            oacc_ref[vrow, :] += q_ref[idx, pl.ds(hk, 1), :] * s_new
            return carry

        lax.fori_loop(0, dk, body, 0)

    run(0, a_gla.T, gqk[:, dk:].T, (gqk[:, :dk] * (REC_DK ** -0.5)).T, col(C_GV).T,
        sg_in_ref, sg_out_ref, og_ref)
    run(1, rdec_ref[...], rk.T, rq.T, col(C_RV).T, sr_in_ref, sr_out_ref, or_ref)

    outs = []
    for acc_ref, g_ref, gate, center in ((og_ref, gn_ref, col(C_GR), False), (or_ref, rn_ref, col(C_RG), True)):
        heads = []
        for h in range(N_HEAD):
            o_h = acc_ref[h * REC_DV:(h + 1) * REC_DV, :]
            if center:
                o_h = o_h - jnp.mean(o_h, axis=0, keepdims=True)
            heads.append(o_h * lax.rsqrt(jnp.mean(o_h * o_h, axis=0, keepdims=True) + EPS))
        normed = jnp.concatenate(heads, axis=0) * g_ref[...]
        outs.append(normed.T * _silu(gate))
    o_ref[...] = jnp.concatenate(outs, axis=1).astype(BF16)


def _rec_sample(proj, cos, sin, lw, s_gla, s_ret):
    bd = proj.shape[0]
    dk, dv = N_HEAD * REC_DK, N_HEAD * REC_DV
    to_lanes = lambda s: s.reshape(bd, dk * REC_DV).T
    rdec = jnp.broadcast_to(jnp.repeat(_ret_tables()[4], REC_DK)[:, None], (dk, bd))
    gn_col = lw['gla_norm'].reshape(dv, 1)
    rn_col = lw['ret_norm'].reshape(dv, 1)
    st = jax.ShapeDtypeStruct((dk * REC_DV, bd), F32)
    o, sg, sr = pl.pallas_call(
        _rec_sample_kernel,
        out_shape=[jax.ShapeDtypeStruct((bd, 2 * BRANCH_W), BF16), st, st],
        scratch_shapes=[pltpu.VMEM((2, dk, bd), F32), pltpu.VMEM((2, dk, bd), F32),
                        pltpu.VMEM((2, dk, bd), F32), pltpu.VMEM((2, dv, bd), F32),
                        pltpu.VMEM((dv, bd), F32), pltpu.VMEM((dv, bd), F32)],
        compiler_params=pltpu.CompilerParams(vmem_limit_bytes=VMEM_LIMIT),
        name="rec_sample",
    )(proj, cos, sin, lw['w2'], lw['gla_b'], gn_col, rn_col, rdec, to_lanes(s_gla), to_lanes(s_ret))
    from_lanes = lambda s: s.T.reshape(bd, N_HEAD, REC_DK, REC_DV)
    return o, from_lanes(sg), from_lanes(sr)


def _prep_kernel(mcq_ref, mckv_ref, tail_ref, sq_ref, sk_ref, sv_ref, cos_ref, sin_ref,
                 nq_ref, nkv_ref, wuq_ref, wuk_ref,
                 qcat_ref, kcat_ref, ckv_ref, krope_ref, sqb_ref, skb_ref, svb_ref):
    cos, sin = cos_ref[...], sin_ref[...]
    c_q = _rms(mcq_ref[...], nq_ref[...]).astype(BF16)
    q_c = _dot(c_q, wuq_ref[...])
    q_lat = _dot(q_c[:, :N_HEAD * MLA_NOPE].astype(BF16), wuk_ref[...])
    q_rope = _rope_lanes(q_c[:, N_HEAD * MLA_NOPE:], cos, sin)
    c_kv = _rms(mckv_ref[...], nkv_ref[...])
    k_rope = _rope_lanes(tail_ref[...], cos, sin)
    lane = _iota(k_rope.shape, 1)
    ckv_ref[...] = c_kv
    krope_ref[...] = k_rope[:, :MLA_DR]
    kcat_ref[:, :MLA_DC] = c_kv.astype(BF16)
    one = jnp.where(lane == MLA_ONE - MLA_DC, 1.0, 0.0)
    kcat_ref[:, MLA_DC:] = jnp.where(lane < MLA_DR, k_rope, one).astype(BF16)
    for h in range(N_HEAD):
        qcat_ref[h, :, :MLA_DC] = q_lat[:, h * MLA_DC:(h + 1) * MLA_DC].astype(BF16)
        shifted = q_rope if h == 0 else pltpu.roll(q_rope, LANE - h * MLA_DR, 1)
        qcat_ref[h, :, MLA_DC:] = jnp.where(lane < MLA_DR, shifted, 0.0).astype(BF16)
    sqb_ref[...] = (sq_ref[...] * (SB_DH ** -0.5)).astype(BF16)
    skb_ref[...] = sk_ref[...].astype(BF16)
    svb_ref[...] = sv_ref[...].astype(BF16)


def _prep(proj, cos, sin, lw):
    m = proj.shape[0]
    tm = min(m, 512)
    blk256 = lambda j: pl.BlockSpec((tm, 256), lambda i, j=j: (i, j))
    blk128 = lambda j: pl.BlockSpec((tm, LANE), lambda i, j=j: (i, j))
    row = lambda w: pl.BlockSpec((tm, w), lambda i: (i, 0))
    full = lambda a: pl.BlockSpec(a.shape, lambda i: (0,) * a.ndim)
    consts = (lw['mla_norm_q'], lw['mla_norm_kv'], lw['w_uq'], lw['w_uk'])
    return pl.pallas_call(
        _prep_kernel,
        grid=(m // tm,),
        in_specs=[blk256(C_MCQ), blk128(C_MCKV), blk128(C_TAIL), blk256(C_SQ), blk256(C_SK), blk256(C_SV),
                  row(LANE), row(LANE)] + [full(a) for a in consts],
        out_specs=[pl.BlockSpec((N_HEAD, tm, 256), lambda i: (0, i, 0)), row(256), row(MLA_DC), row(MLA_DR),
                   row(256), row(256), row(256)],
        out_shape=[jax.ShapeDtypeStruct((N_HEAD, m, 256), BF16), jax.ShapeDtypeStruct((m, 256), BF16),
                   jax.ShapeDtypeStruct((m, MLA_DC), F32), jax.ShapeDtypeStruct((m, MLA_DR), F32),
                   jax.ShapeDtypeStruct((m, 256), BF16), jax.ShapeDtypeStruct((m, 256), BF16),
                   jax.ShapeDtypeStruct((m, 256), BF16)],
        compiler_params=_cparams(("arbitrary",)),
        name="mixer_prep",
    )(proj, proj, proj, proj, proj, proj, cos, sin, *consts)


_EXP2_SCALE = MLA_SCALE * math.log2(math.e)


def _mla_prompt_kernel(q_ref, k_ref, wuv_ref, o_ref, m_ref, acc_ref, *, tq, tk):
    i = pl.program_id(0)
    m_ref[...] = jnp.full_like(m_ref, -jnp.inf)
    acc_ref[...] = jnp.zeros_like(acc_ref)
    reps = tk // LANE

    def tile(j, masked):
        k = k_ref[pl.ds(pl.multiple_of(j * tk, tk), tk), :]
        if masked:
            keep = (j * tk + _iota((tq, tk), 1)) <= (i * tq + _iota((tq, tk), 0))
        for h in range(N_HEAD):
            rs = slice(h * tq, (h + 1) * tq)
            s = _dot_nt(q_ref[h], k)
            if masked:
                s = jnp.where(keep, s, -jnp.inf)
            m_old = m_ref[rs, :]
            m_new = jnp.maximum(m_old, jnp.max(s, axis=-1, keepdims=True))
            alpha = jnp.exp2((m_old - m_new) * _EXP2_SCALE)
            p = jnp.exp2((s - jnp.tile(m_new, (1, reps))) * _EXP2_SCALE)
            acc_ref[rs, :] = jnp.tile(alpha, (1, 2)) * acc_ref[rs, :] + _dot(p.astype(BF16), k)
            m_ref[rs, :] = m_new

    n_full = (i * tq) // tk

    def body(j, carry):
        tile(j, False)
        return carry

    lax.fori_loop(0, n_full, body, 0)
    tile(n_full, True)
    out = None
    for h in range(N_HEAD):
        acc = acc_ref[h * tq:(h + 1) * tq, :]
        o_lat = acc[:, :MLA_DC] / acc[:, MLA_ONE:MLA_ONE + 1]
        term = _dot(o_lat.astype(BF16), wuv_ref[h])
        out = term if out is None else out + term
    o_ref[...] = out.astype(BF16)


def _mla_prompt(qcat, kcat, wuv):
    l = kcat.shape[0]
    tq = min(l, MLA_TQ)
    tk = min(l, MLA_TK)
    rows = N_HEAD * tq
    return pl.pallas_call(
        functools.partial(_mla_prompt_kernel, tq=tq, tk=tk),
        grid=(l // tq,),
        in_specs=[pl.BlockSpec((N_HEAD, tq, 256), lambda i: (0, i, 0)),
                  pl.BlockSpec((l, 256), lambda i: (0, 0)),
                  pl.BlockSpec(wuv.shape, lambda i: (0, 0, 0))],
        out_specs=pl.BlockSpec((tq, BRANCH_W), lambda i: (i, 0)),
        out_shape=jax.ShapeDtypeStruct((l, BRANCH_W), BF16),
        scratch_shapes=[pltpu.VMEM((rows, LANE), F32), pltpu.VMEM((rows, 256), F32)],
        compiler_params=_cparams(("arbitrary",)),
        name="mla_prompt",
    )(qcat, kcat, wuv)


def _strict_suffix_matrix(n):
    return (_iota((n, n), 0) > _iota((n, n), 1)).astype(BF16)


def _sb_block(q_h, k_h, v_h, carry, mask, suffix):
    z = _dot_nt(q_h, k_h)
    lsig, l1m = _log_sigmoid_pair(z)
    if mask is not None:
        l1m = jnp.where(mask, l1m, 0.0)
    cum = _dot_exact_rhs(l1m, suffix, terms=2)
    a = jnp.exp(lsig + cum + carry)
    if mask is not None:
        a = jnp.where(mask, a, 0.0)
    return _dot(a.astype(BF16), v_h), carry + jnp.sum(l1m, axis=-1, keepdims=True)


def _sb_prompt_kernel(q_ref, k_ref, v_ref, o_ref, *, tq):
    i = pl.program_id(0)
    suffix = _strict_suffix_matrix(tq)
    q = q_ref[...]
    qpos = i * tq + _iota((tq, tq), 0)

    def cond(state):
        return state[0]

    def body(state):
        _, j, carries, accs = state
        rows = pl.ds(pl.multiple_of(j * tq, tq), tq)
        mask = (j * tq + _iota((tq, tq), 1)) < qpos
        new_c, new_a, alive = [], [], None
        for h in range(N_HEAD):
            hs = slice(h * SB_DH, (h + 1) * SB_DH)
            pv, c_h = _sb_block(q[:, hs], k_ref[rows, hs], v_ref[rows, hs], carries[h], mask, suffix)
            new_c.append(c_h)
            new_a.append(accs[h] + pv)
            top = jnp.max(c_h)
            alive = top if alive is None else jnp.maximum(alive, top)
        go = jnp.logical_and(j > 0, alive > SB_DEAD)
        return go, j - 1, tuple(new_c), tuple(new_a)

    init = (i >= 0, i, tuple(jnp.zeros((tq, 1), F32) for _ in range(N_HEAD)),
            tuple(jnp.zeros((tq, SB_DH), F32) for _ in range(N_HEAD)))
    _, _, _, accs = lax.while_loop(cond, body, init)
    o_ref[...] = jnp.concatenate(accs, axis=1).astype(BF16)


def _sb_prompt(sq, sk, sv):
    l = sq.shape[0]
    tq = min(l, SB_TQ)
    full = pl.BlockSpec((l, 256), lambda i: (0, 0))
    return pl.pallas_call(
        functools.partial(_sb_prompt_kernel, tq=tq),
        grid=(l // tq,),
        in_specs=[pl.BlockSpec((tq, 256), lambda i: (i, 0)), full, full],
        out_specs=pl.BlockSpec((tq, BRANCH_W), lambda i: (i, 0)),
        out_shape=jax.ShapeDtypeStruct((l, BRANCH_W), BF16),
        compiler_params=_cparams(("arbitrary",)),
        name="sb_prompt",
    )(sq, sk, sv)


def _mla_decode_kernel(pt_ref, q_ref, kself_ref, wuv_ref, ckv_hbm, kr_hbm, o_ref,
                       ckv_buf, kr_buf, sem, *, layer, n_pages, chunk, page):
    s = pl.program_id(0)
    n_seq = pl.num_programs(0)
    n_chunk = n_pages // chunk

    def copies(g, slot):
        seq, c = g // n_chunk, g % n_chunk
        out = []
        for p in range(chunk):
            pid = pt_ref[seq, c * chunk + p]
            dst = pl.ds(p * page, page)
            out.append(pltpu.make_async_copy(ckv_hbm.at[layer, pid], ckv_buf.at[slot, dst], sem.at[0, slot]))
            out.append(pltpu.make_async_copy(kr_hbm.at[layer, pid], kr_buf.at[slot, :, dst], sem.at[1, slot]))
        return out

    total = n_seq * n_chunk

    @pl.when(s == 0)
    def _():
        for g0 in range(MLA_DECODE_SLOTS - 1):
            for cp in copies(jnp.minimum(g0, total - 1), g0):
                cp.start()

    q = q_ref[0]
    q_lat, q_rope = q[:, :MLA_DC], q[:, MLA_DC:MLA_DC + MLA_DR]
    nq = q.shape[0]
    ahead = MLA_DECODE_SLOTS - 1

    def attend(g, state):
        m_old, l_old, acc = state
        slot = g % MLA_DECODE_SLOTS
        for cp in copies(g, slot):
            cp.wait()
        kc = ckv_buf[slot].astype(BF16)
        sc = (_dot_nt(q_lat, kc) + _dot(q_rope, kr_buf[slot].astype(BF16))) * MLA_SCALE
        m_new = jnp.maximum(m_old, jnp.max(sc, axis=-1, keepdims=True))
        alpha = jnp.exp(m_old - m_new)
        p = jnp.exp(sc - m_new)
        return (m_new, alpha * l_old + jnp.sum(p, axis=-1, keepdims=True),
                alpha * acc + _dot(p.astype(BF16), kc))

    def refill(g):
        nxt = g + ahead
        for cp in copies(jnp.minimum(nxt, total - 1), nxt % MLA_DECODE_SLOTS):
            cp.start()

    def step(c, state):
        g = s * n_chunk + c
        state = attend(g, state)
        refill(g)
        return state

    state = (jnp.full((nq, 1), -jnp.inf, F32), jnp.zeros((nq, 1), F32), jnp.zeros((nq, MLA_DC), F32))
    m_old, l_old, acc = lax.fori_loop(0, n_chunk, step, state)

    @pl.when(s == n_seq - 1)
    def _():
        for extra in range(ahead):
            for cp in copies(total - 1, (total + extra) % MLA_DECODE_SLOTS):
                cp.wait()

    kself = kself_ref[0].astype(F32)
    s_self = jnp.sum(q.astype(F32) * kself, axis=-1, keepdims=True) * MLA_SCALE
    m_new = jnp.maximum(m_old, s_self)
    alpha = jnp.exp(m_old - m_new)
    p_self = jnp.exp(s_self - m_new).astype(BF16).astype(F32)
    l_new = alpha * l_old + p_self
    o_lat = ((alpha * acc + p_self * kself[:, :MLA_DC]) / l_new).astype(BF16)
    out = None
    for h in range(N_HEAD):
        term = jnp.where(_iota((nq, BRANCH_W), 0) == h, _dot(o_lat, wuv_ref[h]), 0.0)
        out = term if out is None else out + term
    o_ref[0] = jnp.sum(out, axis=0, keepdims=True).astype(BF16)


def _mla_decode(page_table, q16, kself, wuv, cache_ckv, cache_kr_t, layer):
    bd, n_pages = page_table.shape
    page = cache_ckv.shape[2]
    chunk = math.gcd(n_pages, MLA_DECODE_CHUNK)
    grid_spec = pltpu.PrefetchScalarGridSpec(
        num_scalar_prefetch=1,
        grid=(bd,),
        in_specs=[pl.BlockSpec((1,) + q16.shape[1:], lambda s, pt: (s, 0, 0)),
                  pl.BlockSpec((1, 1, 256), lambda s, pt: (s, 0, 0)),
                  pl.BlockSpec(wuv.shape, lambda s, pt: (0, 0, 0)),
                  pl.BlockSpec(memory_space=pl.ANY),
                  pl.BlockSpec(memory_space=pl.ANY)],
        out_specs=pl.BlockSpec((1, 1, BRANCH_W), lambda s, pt: (s, 0, 0)),
        scratch_shapes=[pltpu.VMEM((MLA_DECODE_SLOTS, chunk * page, MLA_DC), F32),
                        pltpu.VMEM((MLA_DECODE_SLOTS, MLA_DR, chunk * page), F32),
                        pltpu.SemaphoreType.DMA((2, MLA_DECODE_SLOTS))],
    )
    out = pl.pallas_call(
        functools.partial(_mla_decode_kernel, layer=layer, n_pages=n_pages, chunk=chunk, page=page),
        grid_spec=grid_spec,
        out_shape=jax.ShapeDtypeStruct((bd, 1, BRANCH_W), BF16),
        compiler_params=_cparams(("arbitrary",)),
        name="mla_decode",
    )(page_table, q16, kself, wuv, cache_ckv, cache_kr_t)
    return out.reshape(bd, BRANCH_W)


def _sb_decode_kernel(pt_ref, q_ref, k_hbm, v_hbm, o_ref, kbuf, vbuf, kx, vx, sem, semx,
                      *, layer, n_pages, group, page):
    blk = pl.program_id(0)
    suffix = _strict_suffix_matrix(page)

    def head_copies(g, j):
        pid = pt_ref[blk * group + g, n_pages - 1 - j]
        return (pltpu.make_async_copy(k_hbm.at[layer, pid], kbuf.at[g, j], sem.at[0, g, j]),
                pltpu.make_async_copy(v_hbm.at[layer, pid], vbuf.at[g, j], sem.at[1, g, j]))

    for g in range(group):
        for j in range(2):
            for cp in head_copies(g, j):
                cp.start()

    def page_step(q8, kref, vref, carry, accs):
        z = None
        for h in range(N_HEAD):
            term = _dot(q8[h], kref[h].astype(BF16))
            z = term if z is None else z + term
        lsig, l1m = _log_sigmoid_pair(z)
        cum = _dot_exact_rhs(l1m, suffix, terms=2)
        a = jnp.exp(lsig + cum + carry).astype(BF16)
        new_accs = tuple(accs[h] + _dot_nt(a, vref[h].astype(BF16)) for h in range(N_HEAD))
        return carry + jnp.sum(l1m, axis=-1, keepdims=True), new_accs

    rows8 = _iota((8, SB_DH), 0)
    for g in range(group):
        seq = blk * group + g
        qrow = q_ref[g].astype(F32)
        q8 = [jnp.where(rows8 == h, jnp.broadcast_to(qrow[:, h * SB_DH:(h + 1) * SB_DH], (8, SB_DH)),
                        0.0).astype(BF16) for h in range(N_HEAD)]
        carry = jnp.zeros((8, 1), F32)
        accs = tuple(jnp.zeros((8, SB_DH), F32) for _ in range(N_HEAD))
        for cp in head_copies(g, 0):
            cp.wait()
        carry, accs = page_step(q8, kbuf.at[g, 0], vbuf.at[g, 0], carry, accs)
        for cp in head_copies(g, 1):
            cp.wait()
        carry, accs = page_step(q8, kbuf.at[g, 1], vbuf.at[g, 1], carry, accs)

        def cond(state):
            return state[0]

        def body(state):
            _, j, c, a = state
            pid = pt_ref[seq, n_pages - 1 - j]
            ck = pltpu.make_async_copy(k_hbm.at[layer, pid], kx, semx.at[0])
            cv = pltpu.make_async_copy(v_hbm.at[layer, pid], vx, semx.at[1])
            ck.start()
            cv.start()
            ck.wait()
            cv.wait()
            c, a = page_step(q8, kx, vx, c, a)
            go = jnp.logical_and(j + 1 < n_pages, jnp.max(c[:N_HEAD]) > SB_DEAD)
            return go, j + 1, c, a

        go0 = jnp.logical_and(n_pages > 2, jnp.max(carry[:N_HEAD]) > SB_DEAD)
        _, _, carry, accs = lax.while_loop(cond, body, (go0, jnp.int32(2), carry, accs))
        out = jnp.concatenate(accs, axis=1)
        keep = _iota(out.shape, 0) == _iota(out.shape, 1) // SB_DH
        o_ref[g] = jnp.sum(jnp.where(keep, out, 0.0), axis=0, keepdims=True).astype(BF16)


def _sb_decode(page_table, sq, cache_k_t, cache_v_t, layer):
    bd, n_pages = page_table.shape
    assert n_pages >= 2
    page = cache_k_t.shape[4]
    group = math.gcd(bd, 8)
    slab = (N_HEAD, SB_DH, page)
    grid_spec = pltpu.PrefetchScalarGridSpec(
        num_scalar_prefetch=1,
        grid=(bd // group,),
        in_specs=[pl.BlockSpec((group, 1, 256), lambda b, pt: (b, 0, 0)),
                  pl.BlockSpec(memory_space=pl.ANY),
                  pl.BlockSpec(memory_space=pl.ANY)],
        out_specs=pl.BlockSpec((group, 1, BRANCH_W), lambda b, pt: (b, 0, 0)),
        scratch_shapes=[pltpu.VMEM((group, 2) + slab, F32), pltpu.VMEM((group, 2) + slab, F32),
                        pltpu.VMEM(slab, F32), pltpu.VMEM(slab, F32),
                        pltpu.SemaphoreType.DMA((2, group, 2)), pltpu.SemaphoreType.DMA((2,))],
    )
    out = pl.pallas_call(
        functools.partial(_sb_decode_kernel, layer=layer, n_pages=n_pages, group=group, page=page),
        grid_spec=grid_spec,
        out_shape=jax.ShapeDtypeStruct((bd, 1, BRANCH_W), BF16),
        compiler_params=_cparams(("arbitrary",)),
        name="sb_decode",
    )(page_table, sq.reshape(bd, 1, 256), cache_k_t, cache_v_t)
    return out.reshape(bd, BRANCH_W)


def _merge_kernel(x_ref, oab_ref, oc_ref, od_ref, gpre_ref, gpost_ref, wg_ref, wb_ref, wo_ref, y_ref):
    x = x_ref[...]
    h = _rms(x, gpre_ref[...]).astype(BF16)
    oab = oab_ref[...]
    branches = (oab[:, :BRANCH_W], oab[:, BRANCH_W:], oc_ref[...], od_ref[...])
    d = x.shape[1]
    merged = None
    for n in range(N_BRANCH):
        gate = jax.nn.sigmoid(_dot(h, wg_ref[:, n * d:(n + 1) * d]))
        term = gate * _dot(branches[n], wb_ref[n])
        merged = term if merged is None else merged + term
    mix = _dot(merged.astype(BF16), wo_ref[...])
    y_ref[...] = x + _rms(mix, gpost_ref[...])


def _merge(x, oab, oc, od, lw):
    m, d = x.shape
    tm = min(m, 512)
    row = lambda w: pl.BlockSpec((tm, w), lambda i: (i, 0))
    full = lambda a: pl.BlockSpec(a.shape, lambda i: (0,) * a.ndim)
    consts = (lw['norm_mix_pre'], lw['norm_mix_post'], lw['w_gate'], lw['w_branch'], lw['w_out'])
    return pl.pallas_call(
        _merge_kernel,
        grid=(m // tm,),
        in_specs=[row(d), row(2 * BRANCH_W), row(BRANCH_W), row(BRANCH_W)] + [full(a) for a in consts],
        out_specs=row(d),
        out_shape=jax.ShapeDtypeStruct((m, d), F32),
        compiler_params=_cparams(("arbitrary",)),
        name="merge",
    )(x, oab, oc, od, *consts)


def _gelu(x):
    return 0.5 * x * (1.0 + lax.erf(x * (2.0 ** -0.5)))


def _ffn_tail(c, x_ref, gpost_ref, acc_ref, y_ref):
    @pl.when(c == pl.num_programs(1) - 1)
    def _():
        y_ref[...] = x_ref[...] + _rms(acc_ref[...], gpost_ref[...])


def _ffn_prompt_kernel(x_ref, gpre_ref, gpost_ref, wg_ref, wu_ref, cw_ref, cb_ref, wo_ref,
                       y_ref, tail_ref, h_ref, acc_ref, prev_ref):
    i, c = pl.program_id(0), pl.program_id(1)

    @pl.when(c == 0)
    def _():
        h_ref[...] = _rms(x_ref[...], gpre_ref[...]).astype(BF16)
        acc_ref[...] = jnp.zeros_like(acc_ref)

    @pl.when(i == 0)
    def _():
        prev_ref[c] = jnp.zeros(prev_ref.shape[1:], F32)

    h = h_ref[...]
    gp = _dot(h, wg_ref[...])
    tm = gp.shape[0]
    prev = prev_ref[c]
    row = _iota(gp.shape, 0)
    back1 = jnp.where(row == 0, prev[7:8], pltpu.roll(gp, 1, 0))
    back2 = jnp.where(row == 0, prev[6:7], jnp.where(row == 1, prev[7:8], pltpu.roll(gp, 2, 0)))
    cw = cw_ref[...]
    conv = cb_ref[...] + cw[0:1] * back2 + cw[1:2] * back1 + cw[2:3] * gp
    act = _gelu(conv) * _dot(h, wu_ref[...])
    acc_ref[...] += _dot(act.astype(BF16), wo_ref[...])
    prev_ref[c] = gp[tm - 8:tm]
    tail_ref[0] = gp[tm - (CONV_W - 1):tm]
    _ffn_tail(c, x_ref, gpost_ref, acc_ref, y_ref)


def _ffn_sample_kernel(x_ref, gpre_ref, gpost_ref, wg_ref, wu_ref, cw_ref, cb_ref, wo_ref, s0_ref, s1_ref,
                       y_ref, gp_ref, h_ref, acc_ref):
    c = pl.program_id(1)

    @pl.when(c == 0)
    def _():
        h_ref[...] = _rms(x_ref[...], gpre_ref[...]).astype(BF16)
        acc_ref[...] = jnp.zeros_like(acc_ref)

    h = h_ref[...]
    gp = _dot(h, wg_ref[...])
    cw = cw_ref[...]
    conv = cb_ref[...] + cw[0:1] * s0_ref[...] + cw[1:2] * s1_ref[...] + cw[2:3] * gp
    act = _gelu(conv) * _dot(h, wu_ref[...])
    acc_ref[...] += _dot(act.astype(BF16), wo_ref[...])
    gp_ref[...] = gp
    _ffn_tail(c, x_ref, gpost_ref, acc_ref, y_ref)


def _ffn(x, lw, state=None):
    m, d = x.shape
    dff = lw['w_ffn_out'].shape[0]
    nck = 2
    ck = dff // nck
    tm = min(m, 512)
    xrow = pl.BlockSpec((tm, d), lambda i, c: (i, 0))
    vec = lambda a: pl.BlockSpec(a.shape, lambda i, c: (0,) * a.ndim)
    in_specs = [xrow, vec(lw['norm_ffn_pre']), vec(lw['norm_ffn_post']),
                pl.BlockSpec((d, ck), lambda i, c: (0, c)),
                pl.BlockSpec((d, ck), lambda i, c: (0, c + nck)),
                pl.BlockSpec((CONV_W, ck), lambda i, c: (0, c)),
                pl.BlockSpec((1, ck), lambda i, c: (0, c)),
                pl.BlockSpec((ck, d), lambda i, c: (c, 0))]
    args = [x, lw['norm_ffn_pre'], lw['norm_ffn_post'], lw['w_ffn_in'], lw['w_ffn_in'],
            lw['ffn_conv_w'], lw['ffn_conv_b'], lw['w_ffn_out']]
    scratch = [pltpu.VMEM((tm, d), BF16), pltpu.VMEM((tm, d), F32)]
    if state is None:
        kern = _ffn_prompt_kernel
        out_specs = [xrow, pl.BlockSpec((1, CONV_W - 1, ck), lambda i, c: (i, 0, c))]
        out_shape = [jax.ShapeDtypeStruct((m, d), F32), jax.ShapeDtypeStruct((m // tm, CONV_W - 1, dff), F32)]
        scratch = scratch + [pltpu.VMEM((nck, 8, ck), F32)]
    else:
        kern = _ffn_sample_kernel
        srow = pl.BlockSpec((tm, ck), lambda i, c: (i, c))
        in_specs += [srow, srow]
        args += [state[:, 0, :], state[:, 1, :]]
        out_specs = [xrow, srow]
        out_shape = [jax.ShapeDtypeStruct((m, d), F32), jax.ShapeDtypeStruct((m, dff), F32)]
    return pl.pallas_call(
        kern,
        grid=(m // tm, nck),
        in_specs=in_specs,
        out_specs=out_specs,
        out_shape=out_shape,
        scratch_shapes=scratch,
        compiler_params=_cparams(("arbitrary", "arbitrary")),
        name="conv_ffn",
    )(*args)


def _rope_tables(pos):
    half = MLA_DR // 2
    inv_freq = jnp.exp(-math.log(ROPE_BASE) * jnp.arange(half, dtype=F32) / half)
    ang = pos.astype(F32)[:, None] * inv_freq[None, :]
    cos, sin = jnp.cos(ang), jnp.sin(ang)
    reps = LANE // MLA_DR
    return (jnp.tile(jnp.concatenate([cos, cos], axis=1), (1, reps)),
            jnp.tile(jnp.concatenate([-sin, sin], axis=1), (1, reps)))


def _layer_weights(l, w_in, gla_w_gate2, gla_b_gate, gla_norm, ret_norm, mla_norm_q, mla_norm_kv, mla_w_uq,
                   mla_w_uk, mla_w_uv, w_branch, w_out, w_ffn_in, ffn_conv_w, ffn_conv_b, w_ffn_out,
                   norm_mix_pre, norm_mix_post, norm_ffn_pre, norm_ffn_post):
    d = w_in.shape[1]
    hk, hv = N_HEAD * REC_DK, N_HEAD * REC_DV
    sizes = (hk, hk, hv, GLA_RANK, hv, hk, hk, hv, hv, MLA_DQ, MLA_DC, MLA_DR,
             N_HEAD * SB_DH, N_HEAD * SB_DH, N_HEAD * SB_DH, N_BRANCH * d)
    pts, acc = [], 0
    for s in sizes[:-1]:
        acc += s
        pts.append(acc)
    (g_q, g_k, g_v, g_lr, g_r, r_q, r_k, r_v, r_g, m_cq, m_ckv, m_kr, s_q, s_k, s_v, w_gate) = jnp.split(
        w_in[l], pts, axis=1)
    pad = jnp.zeros((d, LANE - MLA_DR - GLA_RANK), F32)
    w_mix = jnp.concatenate([g_q, g_k, g_v, g_r, r_q, r_k, r_v, r_g, m_cq, s_q, s_k, s_v, m_ckv,
                             m_kr, g_lr, pad], axis=1)
    assert w_mix.shape[1] == D_MIX
    w2 = jnp.zeros((LANE, hk), F32).at[TAIL_GLR:TAIL_GLR + GLA_RANK].set(gla_w_gate2[l])
    uq = mla_w_uq[l].reshape(MLA_DQ, N_HEAD, MLA_NOPE + MLA_DR)
    w_uq = jnp.concatenate([uq[:, :, :MLA_NOPE].reshape(MLA_DQ, -1), uq[:, :, MLA_NOPE:].reshape(MLA_DQ, -1)], axis=1)
    uk = jnp.transpose(mla_w_uk[l], (1, 2, 0))
    w_uk = jnp.zeros((N_HEAD, MLA_NOPE, N_HEAD, MLA_DC), F32)
    for h in range(N_HEAD):
        w_uk = w_uk.at[h, :, h, :].set(uk[h])
    w_uk = w_uk.reshape(N_HEAD * MLA_NOPE, N_HEAD * MLA_DC)
    uv = jnp.transpose(mla_w_uv[l], (1, 0, 2))
    w_uv = jnp.zeros((N_HEAD, MLA_DC, N_HEAD, MLA_DV), F32)
    for h in range(N_HEAD):
        w_uv = w_uv.at[h, :, h, :].set(uv[h])
    w_uv = w_uv.reshape(N_HEAD, MLA_DC, N_HEAD * MLA_DV)
    row = lambda a: a[l].reshape(1, -1)
    return {
        'w_mix': w_mix.astype(BF16), 'w_gate': w_gate.astype(BF16), 'w2': w2.astype(BF16),
        'gla_b': row(gla_b_gate), 'gla_norm': row(gla_norm), 'ret_norm': row(ret_norm),
        'mla_norm_q': row(mla_norm_q), 'mla_norm_kv': row(mla_norm_kv),
        'w_uq': w_uq.astype(BF16), 'w_uk': w_uk.astype(BF16), 'w_uv': w_uv.astype(BF16),
        'w_branch': w_branch[l].astype(BF16), 'w_out': w_out[l].astype(BF16),
        'w_ffn_in': w_ffn_in[l].astype(BF16), 'ffn_conv_w': ffn_conv_w[l], 'ffn_conv_b': row(ffn_conv_b),
        'w_ffn_out': w_ffn_out[l].astype(BF16),
        'norm_mix_pre': row(norm_mix_pre), 'norm_mix_post': row(norm_mix_post),
        'norm_ffn_pre': row(norm_ffn_pre), 'norm_ffn_post': row(norm_ffn_post),
    }


def kernel(x_prompt, x_sample, cache_mla_ckv, cache_mla_krope, cache_sb_k, cache_sb_v, state_gla, state_ret, state_ffn_conv, page_table, w_in, gla_w_gate2, gla_b_gate, gla_norm, ret_norm, mla_norm_q, mla_norm_kv, mla_w_uq, mla_w_uk, mla_w_uv, w_branch, w_out, w_ffn_in, ffn_conv_w, ffn_conv_b, w_ffn_out, norm_mix_pre, norm_mix_post, norm_ffn_pre, norm_ffn_post):
    bp, seq, d = x_prompt.shape
    bd, dseq, _ = x_sample.shape
    assert bp == 1 and dseq == 1
    depth = w_in.shape[0]
    past = page_table.shape[1] * cache_mla_ckv.shape[2]
    cos_p, sin_p = _rope_tables(jnp.arange(seq, dtype=jnp.int32))
    cos_s, sin_s = _rope_tables(jnp.full((bd,), past, jnp.int32))

    cache_kr_t = jnp.transpose(cache_mla_krope, (0, 1, 3, 2))
    cache_k_t = jnp.transpose(cache_sb_k, (0, 1, 3, 4, 2))
    cache_v_t = jnp.transpose(cache_sb_v, (0, 1, 3, 4, 2))

    xp = x_prompt.reshape(seq, d)
    xs = x_sample.reshape(bd, d)
    p_rows = [[] for _ in range(7)]
    s_rows = [[] for _ in range(7)]
    for l in range(depth):
        lw = _layer_weights(l, w_in, gla_w_gate2, gla_b_gate, gla_norm, ret_norm, mla_norm_q, mla_norm_kv,
                            mla_w_uq, mla_w_uk, mla_w_uv, w_branch, w_out, w_ffn_in, ffn_conv_w, ffn_conv_b,
                            w_ffn_out, norm_mix_pre, norm_mix_post, norm_ffn_pre, norm_ffn_post)
        proj = _norm_matmul(xp, lw['norm_mix_pre'], lw['w_mix'])
        oab, gla_new, ret_new = _rec_prompt(proj, cos_p, sin_p, lw)
        qcat, kcat, ckv, krope, sqb, skb, svb = _prep(proj, cos_p, sin_p, lw)
        oc = _mla_prompt(qcat, kcat, lw['w_uv'])
        od = _sb_prompt(sqb, skb, svb)
        xp = _merge(xp, oab, oc, od, lw)
        xp, conv_tails = _ffn(xp, lw)
        conv_tail = conv_tails[-1]
        sk_cols = slice(C_SK * 256, (C_SK + 1) * 256)
        sv_cols = slice(C_SV * 256, (C_SV + 1) * 256)
        for i, a in enumerate((ckv.reshape(1, seq, MLA_DC), krope.reshape(1, seq, MLA_DR),
                               proj[:, sk_cols].reshape(1, seq, N_HEAD, SB_DH),
                               proj[:, sv_cols].reshape(1, seq, N_HEAD, SB_DH),
                               gla_new[None], ret_new[None], conv_tail[None])):
            p_rows[i].append(a)

        proj = _norm_matmul(xs, lw['norm_mix_pre'], lw['w_mix'])
        oab, gla_new, ret_new = _rec_sample(proj, cos_s, sin_s, lw, state_gla[l], state_ret[l])
        qcat, kcat, ckv, krope, sqb, skb, svb = _prep(proj, cos_s, sin_s, lw)
        q16 = jnp.zeros((bd, 16, 256), BF16).at[:, :N_HEAD].set(jnp.transpose(qcat, (1, 0, 2)))
        oc = _mla_decode(page_table, q16, kcat.reshape(bd, 1, 256), lw['w_uv'], cache_mla_ckv, cache_kr_t, l)
        od = _sb_decode(page_table, sqb, cache_k_t, cache_v_t, l)
        xs = _merge(xs, oab, oc, od, lw)
        xs, gate_pre = _ffn(xs, lw, state_ffn_conv[l])
        conv_new = jnp.stack([state_ffn_conv[l][:, 1, :], gate_pre], axis=1)
        for i, a in enumerate((ckv.reshape(bd, 1, MLA_DC), krope.reshape(bd, 1, MLA_DR),
                               proj[:, sk_cols].reshape(bd, 1, N_HEAD, SB_DH),
                               proj[:, sv_cols].reshape(bd, 1, N_HEAD, SB_DH),
                               gla_new, ret_new, conv_new)):
            s_rows[i].append(a)

    outs_p = [jnp.stack(a, axis=0) for a in p_rows]
    outs_s = [jnp.stack(a, axis=0) for a in s_rows]
    return (xp.reshape(1, seq, d), xs.reshape(bd, 1, d), *outs_p, *outs_s)
```
